```python
import jax, jax.numpy as jnp
from jax import lax
import numpy as np

D_MODEL = 2048
BATCH = 1
SEQ = 8192
DEPTH = 1
DEC_BATCH = 32
DEC_SEQ = 32
PAST_LEN = 2048

CHUNK = 64
EPS = 1e-6
RET_HEADS = 4
RET_DK = 128
RET_DV = 256
RET_QK_W = RET_HEADS * RET_DK
RET_V_W = RET_HEADS * RET_DV
RET_PROJ = 2 * RET_QK_W + 2 * RET_V_W
ROPE_BASE = 10000.0
RWKV_HEAD = 64
RWKV_W = D_MODEL // 2
RWKV_HEADS = RWKV_W // RWKV_HEAD
W_LORA = 64
A_LORA = 64
G_LORA = 160
RWKV_PROJ = 3 * RWKV_W + W_LORA + A_LORA + G_LORA
RWKV_GN_EPS = 64e-5
D_IN = RET_PROJ + RWKV_PROJ
D_MIX = RET_V_W + RWKV_W
PEER_HEADS = 8
N_KEYS = 128
N_EXPERTS = N_KEYS * N_KEYS
PEER_DK = 256
PEER_HALF = PEER_DK // 2
PEER_TOPK = 16
PEER_BLOCK = 128

kernel_name = 'hybrid_retention_rwkv7_peer_stream'


def rms_norm(x, g):
    xf = x.astype(jnp.float32)
    y = xf * lax.rsqrt(jnp.mean(xf * xf, -1, keepdims=True) + EPS)
    return (y * g.astype(jnp.float32)).astype(x.dtype)


def rotary(x, pos):
    half = x.shape[-1] // 2
    inv = ROPE_BASE ** (-jnp.arange(half, dtype=jnp.float32) / half)
    ang = pos.astype(jnp.float32)[:, None] * inv[None, :]
    cos = jnp.cos(ang)[None, :, None, :]
    sin = jnp.sin(ang)[None, :, None, :]
    xf = x.astype(jnp.float32)
    x1, x2 = xf[..., :half], xf[..., half:]
    return jnp.concatenate([x1 * cos - x2 * sin, x2 * cos + x1 * sin], -1)


def retention_chunk(s, q, k, v, log_gamma):
    L = q.shape[1]
    idx = jnp.arange(L, dtype=jnp.float32)
    d_intra = jnp.exp(log_gamma[:, None, None] * jnp.abs(idx[:, None] - idx[None, :]))
    scores = jnp.einsum('bihd,bjhd->bhij', q, k) * d_intra[None]
    o_intra = jnp.einsum('bhij,bjhe->bihe', scores, v)
    q_decay = jnp.exp(log_gamma[None, :] * (idx[:, None] + 1.0))
    o_cross = jnp.einsum('bihd,bhde->bihe', q * q_decay[None, :, :, None], s)
    k_decay = jnp.exp(log_gamma[None, :] * (L - 1.0 - idx[:, None]))
    s_new = jnp.exp(log_gamma * L)[None, :, None, None] * s + jnp.einsum('bjhd,bjhe->bhde', k * k_decay[None, :, :, None], v)
    return s_new, o_intra + o_cross


def retention_group(p, pos, s0, gain):
    f32 = jnp.float32
    B, L, _ = p.shape
    q = rotary(p[..., :RET_QK_W].reshape(B, L, RET_HEADS, RET_DK), pos)
    k = rotary(p[..., RET_QK_W:2 * RET_QK_W].reshape(B, L, RET_HEADS, RET_DK), pos) * (RET_DK ** -0.5)
    v = p[..., 2 * RET_QK_W:2 * RET_QK_W + RET_V_W].reshape(B, L, RET_HEADS, RET_DV).astype(f32)
    gate = p[..., 2 * RET_QK_W + RET_V_W:].astype(f32)
    log_gamma = jnp.log1p(-jnp.exp2(-5.0 - jnp.arange(RET_HEADS, dtype=f32)))
    s0 = s0.astype(f32)
    if L <= CHUNK:
        s_new, o = retention_chunk(s0, q, k, v, log_gamma)
    else:
        nc = L // CHUNK
        def to_chunks(t):
            return jnp.moveaxis(t.reshape(B, nc, CHUNK, t.shape[2], t.shape[3]), 1, 0)
        def body(s, qkv):
            return retention_chunk(s, qkv[0], qkv[1], qkv[2], log_gamma)
        s_new, o = lax.scan(body, s0, (to_chunks(q), to_chunks(k), to_chunks(v)))
        o = jnp.moveaxis(o, 0, 1).reshape(B, L, RET_HEADS, RET_DV)
    o = o * lax.rsqrt(jnp.mean(o * o, -1, keepdims=True) + EPS)
    o = o.reshape(B, L, RET_V_W) * gain.astype(f32)
    return (jax.nn.silu(gate) * o).astype(p.dtype), s_new


def rwkv_group(p, shift0, s0, mu, w0, w2, a0, a2, g2, k_k, k_a, r_k, lnx_g, lnx_b):
    f32 = jnp.float32
    B, L, _ = p.shape
    pf = p.astype(f32)
    prev = jnp.concatenate([shift0.astype(f32), pf[:, :-1]], axis=1)
    pm = pf + (prev - pf) * mu.astype(f32)
    o1 = RWKV_W
    o2 = 2 * RWKV_W
    o3 = 3 * RWKV_W
    o4 = o3 + W_LORA
    o5 = o4 + A_LORA
    r, k, v, xw, xa, xg = pm[..., :o1], pm[..., o1:o2], pm[..., o2:o3], pm[..., o3:o4], pm[..., o4:o5], pm[..., o5:]
    w = -jax.nn.softplus(-(w0.astype(f32) + jnp.tanh(xw) @ w2.astype(f32))) - 0.5
    decay = jnp.exp(-jnp.exp(w))
    a = jax.nn.sigmoid(a0.astype(f32) + xa @ a2.astype(f32))
    g = jax.nn.sigmoid(xg) @ g2.astype(f32)
    def heads(t):
        return t.reshape(B, L, RWKV_HEADS, RWKV_HEAD)
    kk = heads(k * k_k.astype(f32))
    kk = kk / jnp.maximum(jnp.sqrt(jnp.sum(kk * kk, -1, keepdims=True)), 1e-12)
    k = k * (1.0 + (a - 1.0) * k_a.astype(f32))
    r_h, k_h, v_h, w_h, a_h = heads(r), heads(k), heads(v), heads(decay), heads(a)
    def step(s, inp):
        r_t, w_t, k_t, v_t, a_t, b_t = inp
        s = (s * w_t[:, :, None, :]
             + jnp.einsum('bhvk,bhk->bhv', s, a_t)[..., None] * b_t[:, :, None, :]
             + v_t[..., None] * k_t[:, :, None, :])
        return s, jnp.einsum('bhvk,bhk->bhv', s, r_t)
    def tm(t):
        return jnp.moveaxis(t, 1, 0)
    s_new, y = lax.scan(step, s0.astype(f32), (tm(r_h), tm(w_h), tm(k_h), tm(v_h), tm(-kk), tm(kk * a_h)))
    y = jnp.moveaxis(y, 0, 1)
    mean = jnp.mean(y, -1, keepdims=True)
    var = jnp.mean(jnp.square(y - mean), -1, keepdims=True)
    y = ((y - mean) * lax.rsqrt(var + RWKV_GN_EPS)).reshape(B, L, RWKV_W) * lnx_g.astype(f32) + lnx_b.astype(f32)
    bonus = jnp.sum(r_h * k_h * r_k.astype(f32), -1, keepdims=True) * v_h
    y = (y + bonus.reshape(B, L, RWKV_W)) * g
    return y.astype(p.dtype), s_new, p[:, -1:]


def peer(h, w_query, sub_keys, expert_u, expert_v):
    f32 = jnp.float32
    B, L, D = h.shape
    t = h.reshape(-1, D)
    T = t.shape[0]
    q = (t @ w_query).reshape(T, PEER_HEADS, 2, PEER_HALF)
    s = jnp.einsum('thpd,hpkd->thpk', q, sub_keys).astype(f32)
    vals, idx = lax.top_k(s, PEER_TOPK)
    cand = vals[:, :, 0, :, None] + vals[:, :, 1, None, :]
    top_vals, top_pos = lax.top_k(cand.reshape(T, PEER_HEADS, PEER_TOPK * PEER_TOPK), PEER_TOPK)
    i1 = jnp.take_along_axis(idx[:, :, 0], top_pos // PEER_TOPK, -1)
    i2 = jnp.take_along_axis(idx[:, :, 1], top_pos % PEER_TOPK, -1)
    expert = i1 * N_KEYS + i2
    gate = jax.nn.softmax(top_vals, -1)
    pad = (-T) % PEER_BLOCK
    nb = (T + pad) // PEER_BLOCK
    t_b = jnp.pad(t, ((0, pad), (0, 0))).reshape(nb, PEER_BLOCK, D)
    e_b = jnp.pad(expert, ((0, pad), (0, 0), (0, 0))).reshape(nb, PEER_BLOCK, PEER_HEADS, PEER_TOPK)
    g_b = jnp.pad(gate, ((0, pad), (0, 0), (0, 0))).reshape(nb, PEER_BLOCK, PEER_HEADS, PEER_TOPK)
    def block(args):
        tb, eb, gb = args
        u = expert_u[eb]
        act = jax.nn.gelu(jnp.einsum('thkd,td->thk', u, tb).astype(f32), approximate=False)
        return jnp.einsum('thk,thkd->td', (gb * act).astype(tb.dtype), expert_v[eb])
    out = lax.map(block, (t_b, e_b, g_b))
    return out.reshape(-1, D)[:T].reshape(B, L, D)


def trunk_layer(x, pos, s_ret, s_rwkv, s_shift, lw):
    (norm1_g, w_in, ret_gain, mu, w0, w2, a0, a2, g2, k_k, k_a, r_k,
     lnx_g, lnx_b, w_out, norm2_g, w_query, sub_keys, expert_u, expert_v) = lw
    h = rms_norm(x, norm1_g)
    p = h @ w_in
    y_ret, s_ret_new = retention_group(p[..., :RET_PROJ], pos, s_ret, ret_gain)
    y_rwkv, s_rwkv_new, shift_new = rwkv_group(p[..., RET_PROJ:], s_shift, s_rwkv, mu, w0, w2, a0, a2, g2,
                                               k_k, k_a, r_k, lnx_g, lnx_b)
    x = x + jnp.concatenate([y_ret, y_rwkv], -1) @ w_out
    x = x + peer(rms_norm(x, norm2_g), w_query, sub_keys, expert_u, expert_v)
    return x, s_ret_new.astype(x.dtype), s_rwkv_new.astype(x.dtype), shift_new


def setup_inputs(seed: int = 0) -> dict:
    key = jax.random.key(seed)
    ks = jax.random.split(key, 32)
    f32 = jnp.float32
    def nrm(k, shape, scale):
        return jax.random.normal(k, shape, f32) * scale
    return {
        'x_prompt': nrm(ks[0], (BATCH, SEQ, D_MODEL), 1.0),
        'x_sample': nrm(ks[1], (DEC_BATCH, DEC_SEQ, D_MODEL), 1.0),
        'state_ret': nrm(ks[2], (DEPTH, DEC_BATCH, RET_HEADS, RET_DK, RET_DV), 0.5),
        'state_rwkv': nrm(ks[3], (DEPTH, DEC_BATCH, RWKV_HEADS, RWKV_HEAD, RWKV_HEAD), 0.5),
        'state_rwkv_shift': nrm(ks[4], (DEPTH, DEC_BATCH, 1, RWKV_PROJ), 1.0),
        'norm1_g': 1.0 + nrm(ks[5], (DEPTH, D_MODEL), 0.02),
        'w_in': nrm(ks[6], (DEPTH, D_MODEL, D_IN), D_MODEL ** -0.5),
        'ret_gain': 1.0 + nrm(ks[7], (DEPTH, RET_V_W), 0.02),
        'rwkv_mu': jax.random.uniform(ks[8], (DEPTH, RWKV_PROJ), f32),
        'rwkv_w0': jax.random.uniform(ks[9], (DEPTH, RWKV_W), f32, -6.0, 0.0),
        'rwkv_w2': nrm(ks[10], (DEPTH, W_LORA, RWKV_W), 0.5 * W_LORA ** -0.5),
        'rwkv_a0': nrm(ks[11], (DEPTH, RWKV_W), 0.1),
        'rwkv_a2': nrm(ks[12], (DEPTH, A_LORA, RWKV_W), 0.5 * A_LORA ** -0.5),
        'rwkv_g2': nrm(ks[13], (DEPTH, G_LORA, RWKV_W), G_LORA ** -0.5),
        'rwkv_k_k': 0.85 + nrm(ks[14], (DEPTH, RWKV_W), 0.05),
        'rwkv_k_a': 1.0 + nrm(ks[15], (DEPTH, RWKV_W), 0.05),
        'rwkv_r_k': nrm(ks[16], (DEPTH, RWKV_HEADS, RWKV_HEAD), 0.1),
        'lnx_g': 1.0 + nrm(ks[17], (DEPTH, RWKV_W), 0.02),
        'lnx_b': nrm(ks[18], (DEPTH, RWKV_W), 0.02),
        'w_out': nrm(ks[19], (DEPTH, D_MIX, D_MODEL), D_MIX ** -0.5),
        'norm2_g': 1.0 + nrm(ks[20], (DEPTH, D_MODEL), 0.02),
        'peer_w_query': nrm(ks[21], (DEPTH, D_MODEL, PEER_HEADS * PEER_DK), D_MODEL ** -0.5),
        'peer_sub_keys': nrm(ks[22], (DEPTH, PEER_HEADS, 2, N_KEYS, PEER_HALF), PEER_HALF ** -0.5),
        'peer_u': nrm(ks[23], (DEPTH, N_EXPERTS, D_MODEL), D_MODEL ** -0.5),
        'peer_v': nrm(ks[24], (DEPTH, N_EXPERTS, D_MODEL), PEER_HEADS ** -0.5),
        'final_g': 1.0 + nrm(ks[25], (D_MODEL,), 0.02),
    }


def reference(x_prompt, x_sample, state_ret, state_rwkv, state_rwkv_shift, norm1_g, w_in, ret_gain,
              rwkv_mu, rwkv_w0, rwkv_w2, rwkv_a0, rwkv_a2, rwkv_g2, rwkv_k_k, rwkv_k_a, rwkv_r_k,
              lnx_g, lnx_b, w_out, norm2_g, peer_w_query, peer_sub_keys, peer_u, peer_v, final_g):
    B = x_prompt.shape[0]
    pos_p = jnp.arange(x_prompt.shape[1], dtype=jnp.int32)
    pos_s = PAST_LEN + jnp.arange(x_sample.shape[1], dtype=jnp.int32)
    hp, hs = x_prompt, x_sample
    ret_p, rwkv_p, shift_p, ret_s, rwkv_s, shift_s = [], [], [], [], [], []
    for l in range(DEPTH):
        lw = (norm1_g[l], w_in[l], ret_gain[l], rwkv_mu[l], rwkv_w0[l], rwkv_w2[l], rwkv_a0[l], rwkv_a2[l],
              rwkv_g2[l], rwkv_k_k[l], rwkv_k_a[l], rwkv_r_k[l], lnx_g[l], lnx_b[l], w_out[l], norm2_g[l],
              peer_w_query[l], peer_sub_keys[l], peer_u[l], peer_v[l])
        hp, sr, sw, sh = trunk_layer(hp, pos_p,
                                     jnp.zeros((B, RET_HEADS, RET_DK, RET_DV), jnp.float32),
                                     jnp.zeros((B, RWKV_HEADS, RWKV_HEAD, RWKV_HEAD), jnp.float32),
                                     jnp.zeros((B, 1, RWKV_PROJ), x_prompt.dtype), lw)
        ret_p.append(sr)
        rwkv_p.append(sw)
        shift_p.append(sh)
        hs, sr, sw, sh = trunk_layer(hs, pos_s, state_ret[l], state_rwkv[l], state_rwkv_shift[l], lw)
        ret_s.append(sr)
        rwkv_s.append(sw)
        shift_s.append(sh)
    y_prompt = rms_norm(hp, final_g)
    y_sample = rms_norm(hs, final_g)
    new_ret_prompt = jnp.stack(ret_p)
    new_rwkv_prompt = jnp.stack(rwkv_p)
    new_shift_prompt = jnp.stack(shift_p)
    new_ret_sample = jnp.stack(ret_s)
    new_rwkv_sample = jnp.stack(rwkv_s)
    new_shift_sample = jnp.stack(shift_s)
    return (y_prompt, y_sample, new_ret_prompt, new_rwkv_prompt, new_shift_prompt, new_ret_sample, new_rwkv_sample, new_shift_sample)
```

```python
import functools
import math

import numpy as np
import jax
import jax.numpy as jnp
from jax import lax
from jax.experimental import pallas as pl
from jax.experimental.pallas import tpu as pltpu

f32 = jnp.float32
bf16 = jnp.bfloat16

D_MODEL = 2048
CHUNK = 64
EPS = 1e-6
RET_HEADS = 4
RET_DK = 128
RET_DV = 256
RET_QK_W = RET_HEADS * RET_DK
RET_V_W = RET_HEADS * RET_DV
RET_PROJ = 2 * RET_QK_W + 2 * RET_V_W
ROPE_BASE = 10000.0
RWKV_HEAD = 64
RWKV_W = D_MODEL // 2
RWKV_HEADS = RWKV_W // RWKV_HEAD
RWKV_PAIRS = RWKV_HEADS // 2
W_LORA = 64
A_LORA = 64
G_LORA = 160
RWKV_PROJ = 3 * RWKV_W + W_LORA + A_LORA + G_LORA
RWKV_PROJ_PAD = 3584
RWKV_GN_EPS = 64e-5
PEER_HEADS = 8
N_KEYS = 128
PEER_DK = 256
PEER_HALF = PEER_DK // 2
PEER_TOPK = 16
PEER_SLOTS = PEER_HEADS * PEER_TOPK

LANES = 128
VMEM_LIMIT = 56 * 1024 * 1024


def _dot(a, b):
    return jnp.dot(a.astype(bf16), b.astype(bf16), preferred_element_type=f32)


def _dot_nt(a, b):
    return lax.dot_general(a.astype(bf16), b.astype(bf16), (((1,), (1,)), ((), ())),
                           preferred_element_type=f32)


def _dot_tn(a, b):
    return lax.dot_general(a.astype(bf16), b.astype(bf16), (((0,), (0,)), ((), ())),
                           preferred_element_type=f32)


def _split2(x):
    hi = x.astype(bf16)
    lo = (x - hi.astype(f32)).astype(bf16)
    return hi, lo


def _split3(x):
    hi = x.astype(bf16)
    r1 = x - hi.astype(f32)
    mid = r1.astype(bf16)
    lo = (r1 - mid.astype(f32)).astype(bf16)
    return hi, mid, lo


def _dot_exact_rhs(a, b_bf):
    hi, mid, lo = _split3(a)
    return (jnp.dot(hi, b_bf, preferred_element_type=f32)
            + jnp.dot(mid, b_bf, preferred_element_type=f32)
            + jnp.dot(lo, b_bf, preferred_element_type=f32))


def _dot_exact_lhs(a_bf, b):
    hi, mid, lo = _split3(b)
    return (jnp.dot(a_bf, hi, preferred_element_type=f32)
            + jnp.dot(a_bf, mid, preferred_element_type=f32)
            + jnp.dot(a_bf, lo, preferred_element_type=f32))


def _dot_hi(a, b):
    ah, al = _split2(a)
    bh, bl = _split2(b)
    return (jnp.dot(ah, bh, preferred_element_type=f32)
            + jnp.dot(ah, bl, preferred_element_type=f32)
            + jnp.dot(al, bh, preferred_element_type=f32))


def _inproj_kernel(x_ref, g_ref, w_ref, o_ref, h_ref):
    @pl.when(pl.program_id(1) == 0)
    def _():
        x = x_ref[...]
        ms = jnp.mean(x * x, axis=-1, keepdims=True)
        h_ref[...] = (x * lax.rsqrt(ms + EPS) * g_ref[...]).astype(bf16)

    o_ref[...] = jnp.dot(h_ref[...], w_ref[...], preferred_element_type=f32)


def _inproj(x, g, w_bf, tm, tn):
    T, D = x.shape
    N = w_bf.shape[1]
    return pl.pallas_call(
        _inproj_kernel,
        grid=(T // tm, N // tn),
        in_specs=[
            pl.BlockSpec((tm, D), lambda i, j: (i, 0)),
            pl.BlockSpec((1, D), lambda i, j: (0, 0)),
            pl.BlockSpec((D, tn), lambda i, j: (0, j)),
        ],
        out_specs=pl.BlockSpec((tm, tn), lambda i, j: (i, j)),
        out_shape=jax.ShapeDtypeStruct((T, N), f32),
        scratch_shapes=[pltpu.VMEM((tm, D), bf16)],
        compiler_params=pltpu.CompilerParams(
            dimension_semantics=("parallel", "arbitrary"), vmem_limit_bytes=VMEM_LIMIT),
        name="inproj",
    )(x, g, w_bf)


def _ret_kernel(q_ref, k_ref, v_ref, gt_ref, cs_ref, sn_ref, mask_ref, dec_ref, gain_ref, s0_ref,
                y_ref, sout_ref, s_scr, *, nblk):
    c = pl.program_id(2)

    @pl.when(c == 0)
    def _():
        s_scr[...] = s0_ref[0, 0]

    q = q_ref[...]
    k = k_ref[...]
    cs = cs_ref[...]
    sn = sn_ref[...]
    half = RET_DK // 2
    qr = q * cs + pltpu.roll(q, half, 1) * sn
    kr = (k * cs + pltpu.roll(k, half, 1) * sn) * (RET_DK ** -0.5)
    dec = dec_ref[0]
    qdec = dec[:, 0:1]
    kdec = dec[:, 1:2]
    gblk = dec[0:1, 2:3]
    v = v_ref[...]
    s = s_scr[...]
    sc = _dot_nt(qr, kr) * mask_ref[0]
    o = _dot(sc, v) + _dot(qr * qdec, s)
    s_new = gblk * s + _dot_tn(kr * kdec, v)
    s_scr[...] = s_new
    ms = jnp.mean(o * o, axis=-1, keepdims=True)
    on = o * lax.rsqrt(ms + EPS) * gain_ref[...]
    gt = gt_ref[...]
    y_ref[...] = gt * jax.nn.sigmoid(gt) * on

    @pl.when(c == nblk - 1)
    def _():
        sout_ref[0, 0] = s_new


def _ret_tables(L, cb, pos0, chunked):
    half = RET_DK // 2
    inv = ROPE_BASE ** (-jnp.arange(half, dtype=f32) / half)
    pos = (pos0 + jnp.arange(L, dtype=jnp.int32)).astype(f32)
    ang = pos[:, None] * inv[None, :]
    cos = jnp.cos(ang)
    sin = jnp.sin(ang)
    cs = jnp.concatenate([cos, cos], -1)
    sn = jnp.concatenate([-sin, sin], -1)
    log_gamma = jnp.log1p(-jnp.exp2(-5.0 - jnp.arange(RET_HEADS, dtype=f32)))
    idx = jnp.arange(cb, dtype=f32)
    dist = jnp.abs(idx[:, None] - idx[None, :])
    mask = jnp.exp(log_gamma[:, None, None] * dist[None])
    if chunked:
        ci = jnp.arange(cb) // CHUNK
        mask = jnp.where((ci[None, :] <= ci[:, None])[None], mask, 0.0)
    qdec = jnp.exp(log_gamma[:, None] * (idx[None, :] + 1.0))
    kdec = jnp.exp(log_gamma[:, None] * (cb - 1.0 - idx[None, :]))
    gblk = jnp.broadcast_to(jnp.exp(log_gamma * cb)[:, None], (RET_HEADS, cb))
    dec = jnp.zeros((RET_HEADS, cb, LANES), f32)
    dec = dec.at[:, :, 0].set(qdec).at[:, :, 1].set(kdec).at[:, :, 2].set(gblk)
    return cs, sn, mask, dec


def _retention(p_ret, row0, B, L, cb, pos0, s0, gain):
    nblk = L // cb
    rb0 = row0 // cb
    cs, sn, mask, dec = _ret_tables(L, cb, pos0, chunked=L > CHUNK)
    T = p_ret.shape[0]
    nq = RET_QK_W // RET_DK
    row = lambda b, h, c: rb0 + b * nblk + c
    y, s_out = pl.pallas_call(
        functools.partial(_ret_kernel, nblk=nblk),
        grid=(B, RET_HEADS, nblk),
        in_specs=[
            pl.BlockSpec((cb, RET_DK), lambda b, h, c: (row(b, h, c), h)),
            pl.BlockSpec((cb, RET_DK), lambda b, h, c: (row(b, h, c), nq + h)),
            pl.BlockSpec((cb, RET_DV), lambda b, h, c: (row(b, h, c), nq + h)),
            pl.BlockSpec((cb, RET_DV), lambda b, h, c: (row(b, h, c), 2 * nq + h)),
            pl.BlockSpec((cb, RET_DK), lambda b, h, c: (c, 0)),
            pl.BlockSpec((cb, RET_DK), lambda b, h, c: (c, 0)),
            pl.BlockSpec((1, cb, cb), lambda b, h, c: (h, 0, 0)),
            pl.BlockSpec((1, cb, LANES), lambda b, h, c: (h, 0, 0)),
            pl.BlockSpec((1, RET_DV), lambda b, h, c: (0, h)),
            pl.BlockSpec((1, 1, RET_DK, RET_DV), lambda b, h, c: (b, h, 0, 0)),
        ],
        out_specs=[
            pl.BlockSpec((cb, RET_DV), lambda b, h, c: (b * nblk + c, h)),
            pl.BlockSpec((1, 1, RET_DK, RET_DV), lambda b, h, c: (b, h, 0, 0)),
        ],
        out_shape=[
            jax.ShapeDtypeStruct((B * L, RET_V_W), f32),
            jax.ShapeDtypeStruct((B, RET_HEADS, RET_DK, RET_DV), f32),
        ],
        scratch_shapes=[pltpu.VMEM((RET_DK, RET_DV), f32)],
        compiler_params=pltpu.CompilerParams(
            dimension_semantics=("arbitrary", "arbitrary", "arbitrary"), vmem_limit_bytes=VMEM_LIMIT),
        name=f"retention_{L}",
    )(p_ret, p_ret, p_ret, p_ret, cs, sn, mask, dec, gain, s0)
    return y, s_out


def _rwkv_kernel(p_ref, sh0_ref, s0_ref, mu_ref, w0_ref, a0_ref, w2_ref, a2_ref, g2_ref, kk_ref, ka_ref,
                 rk_ref, lg_ref, lb_ref, bd_ref, y_ref, sout_ref, carry, s_scr, *, nvalid, nchunk):
    C = CHUNK
    c = pl.program_id(1)

    @pl.when(c == 0)
    def _():
        carry[...] = sh0_ref[0]
        s_scr[...] = s0_ref[0]

    if nvalid == C:
        pf = p_ref[...]
    else:
        pf = jnp.concatenate([p_ref[...], jnp.zeros((C - nvalid, RWKV_PROJ_PAD), f32)], axis=0)
    rows = lax.broadcasted_iota(jnp.int32, (C, 1), 0)
    prev = jnp.where(rows == 0, carry[...], pltpu.roll(pf, 1, 0))
    pm = pf + (prev - pf) * mu_ref[...]
    carry[...] = pf[nvalid - 1:nvalid, :]

    o1, o2, o3 = RWKV_W, 2 * RWKV_W, 3 * RWKV_W
    r = pm[:, :o1]
    k = pm[:, o1:o2]
    v = pm[:, o2:o3]
    xwa = pm[:, o3:o3 + W_LORA + A_LORA]
    xg = pm[:, o3 + W_LORA + A_LORA:o3 + W_LORA + A_LORA + 2 * LANES]
    z = w0_ref[...] + _dot_hi(jnp.tanh(xwa), w2_ref[...])
    softplus = jnp.maximum(-z, 0.0) + jnp.log1p(jnp.exp(-jnp.abs(z)))
    ew = jnp.exp(-softplus - 0.5)
    a = jax.nn.sigmoid(a0_ref[...] + _dot_hi(xwa, a2_ref[...]))
    g = _dot_hi(jax.nn.sigmoid(xg), g2_ref[...])
    k2 = k * (1.0 + (a - 1.0) * ka_ref[...])
    kkr = k * kk_ref[...]
    if nvalid != C:
        valid = rows < nvalid
        ew = jnp.where(valid, ew, 0.0)
        k2 = jnp.where(valid, k2, 0.0)
        v = jnp.where(valid, v, 0.0)
        a = jnp.where(valid, a, 0.0)

    ti = lax.broadcasted_iota(jnp.int32, (C, C), 0)
    tj = lax.broadcasted_iota(jnp.int32, (C, C), 1)
    tri = jnp.where(tj <= ti, 1.0, 0.0).astype(bf16)
    cum = _dot_exact_lhs(tri, -ew)
    pt = jnp.exp(cum)
    pprev = jnp.exp(cum + ew)
    pinv = jnp.exp(-cum)
    clast = cum[C - 1:C, :]
    pcinv = jnp.exp(clast - cum)
    pc = jnp.exp(clast)
    rt_all = r * pt
    kt_all = k2 * pinv
    kp_all = k2 * pcinv
    rkk = r * k2 * rk_ref[...]

    lane = lax.broadcasted_iota(jnp.int32, (C, 2 * RWKV_HEAD), 1)
    m0 = lane < RWKV_HEAD
    gi = lax.broadcasted_iota(jnp.int32, (4 * C, 4 * C), 0)
    gj = lax.broadcasted_iota(jnp.int32, (4 * C, 4 * C), 1)
    gil = gi % C
    gjl = gj % C
    gmask = gjl < gil + jnp.where(gi < 2 * C, 0, 1)
    ei = lax.broadcasted_iota(jnp.int32, (2 * C, 2 * C), 0)
    ej = lax.broadcasted_iota(jnp.int32, (2 * C, 2 * C), 1)
    eye = jnp.where(ei == ej, 1.0, 0.0)
    bd = bd_ref[...]

    def stack(x):
        return jnp.concatenate([jnp.where(m0, x, 0.0), jnp.where(m0, 0.0, x)], axis=0)

    def gsum(x):
        hi, lo = _split2(x)
        return jnp.dot(hi, bd, preferred_element_type=f32) + jnp.dot(lo, bd, preferred_element_type=f32)

    nsq = int(math.log2(C))
    for p in range(RWKV_PAIRS):
        ls = slice(p * 2 * RWKV_HEAD, (p + 1) * 2 * RWKV_HEAD)
        kkp = kkr[:, ls]
        nrm = jnp.sqrt(gsum(kkp * kkp))
        kk = kkp / jnp.maximum(nrm, 1e-12)
        ap = a[:, ls]
        bb = kk * ap
        at_s = stack(-kk * pprev[:, ls])
        rt_s = stack(rt_all[:, ls])
        bt_s = stack(bb * pinv[:, ls])
        kt_s = stack(kt_all[:, ls])
        bp_s = stack(bb * pcinv[:, ls])
        kp_s = stack(kp_all[:, ls])
        v_p = v[:, ls]
        v_s = stack(v_p)
        s0 = s_scr[p]

        ar = jnp.concatenate([at_s, rt_s], axis=0)
        bk = jnp.concatenate([bt_s, kt_s], axis=0)
        gm = jnp.where(gmask, _dot_nt(ar, bk), 0.0)
        nd = gm[:2 * C, :2 * C]
        aak = gm[:2 * C, 2 * C:]
        rbk = gm[2 * C:, :]
        arh = _dot_nt(ar, s0)
        rhs = arh[:2 * C] + _dot(aak, v_s)
        tinv = eye + nd
        npow = nd
        for i in range(1, nsq):
            npow = _dot(npow, npow)
            tinv = tinv + _dot(npow, tinv)
        u_s = _dot(tinv, rhs)
        uv = jnp.concatenate([u_s, v_s], axis=0)
        y_s = arh[2 * C:] + _dot(rbk, uv)
        yp = y_s[:C] + y_s[C:]
        s_new = s0 * pc[:, ls] + _dot_tn(uv, jnp.concatenate([bp_s, kp_s], axis=0))
        s_scr[p] = s_new

        mean = gsum(yp) * (1.0 / RWKV_HEAD)
        yc = yp - mean
        var = gsum(yc * yc) * (1.0 / RWKV_HEAD)
        yn = yc * lax.rsqrt(var + RWKV_GN_EPS) * lg_ref[:, ls] + lb_ref[:, ls]
        bonus = gsum(rkk[:, ls]) * v_p
        yo = (yn + bonus) * g[:, ls]
        y_ref[:, ls] = yo[:nvalid]

    @pl.when(c == nchunk - 1)
    def _():
        sout_ref[0] = s_scr[...]


def _pair_states(s):
    B = s.shape[0]
    sp = s.reshape(B, RWKV_PAIRS, 2, RWKV_HEAD, RWKV_HEAD)
    z = jnp.zeros_like(sp[:, :, 0])
    top = jnp.concatenate([sp[:, :, 0], z], -1)
    bot = jnp.concatenate([z, sp[:, :, 1]], -1)
    return jnp.concatenate([top, bot], -2)


def _unpair_states(sp):
    B = sp.shape[0]
    h = RWKV_HEAD
    return jnp.stack([sp[:, :, :h, :h], sp[:, :, h:, h:]], 2).reshape(B, RWKV_HEADS, h, h)


def _rwkv(p_rwkv, row0, B, L, sh0, s0, consts):
    nvalid = min(L, CHUNK)
    nchunk = L // nvalid
    rb0 = row0 // nvalid
    full = lambda shape: pl.BlockSpec(shape, lambda b, c: (0,) * len(shape))
    in_specs = [
        pl.BlockSpec((nvalid, RWKV_PROJ_PAD), lambda b, c: (rb0 + b * nchunk + c, 0)),
        pl.BlockSpec((1, 1, RWKV_PROJ_PAD), lambda b, c: (b, 0, 0)),
        pl.BlockSpec((1, RWKV_PAIRS, 2 * RWKV_HEAD, 2 * RWKV_HEAD), lambda b, c: (b, 0, 0, 0)),
    ] + [full(x.shape) for x in consts]
    y, s_out = pl.pallas_call(
        functools.partial(_rwkv_kernel, nvalid=nvalid, nchunk=nchunk),
        grid=(B, nchunk),
        in_specs=in_specs,
        out_specs=[
            pl.BlockSpec((nvalid, RWKV_W), lambda b, c: (b * nchunk + c, 0)),
            pl.BlockSpec((1, RWKV_PAIRS, 2 * RWKV_HEAD, 2 * RWKV_HEAD), lambda b, c: (b, 0, 0, 0)),
        ],
        out_shape=[
            jax.ShapeDtypeStruct((B * L, RWKV_W), f32),
            jax.ShapeDtypeStruct((B, RWKV_PAIRS, 2 * RWKV_HEAD, 2 * RWKV_HEAD), f32),
        ],
        scratch_shapes=[
            pltpu.VMEM((1, RWKV_PROJ_PAD), f32),
            pltpu.VMEM((RWKV_PAIRS, 2 * RWKV_HEAD, 2 * RWKV_HEAD), f32),
        ],
        compiler_params=pltpu.CompilerParams(
            dimension_semantics=("arbitrary", "arbitrary"), vmem_limit_bytes=VMEM_LIMIT),
        name=f"rwkv7_{L}",
    )(p_rwkv, sh0, _pair_states(s0), *consts)
    return y, _unpair_states(s_out)


def _outproj_kernel(yr_ref, yw_ref, x_ref, wr_ref, ww_ref, g_ref, x1_ref, h_ref):
    x1 = (x_ref[...] + jnp.dot(yr_ref[...].astype(bf16), wr_ref[...], preferred_element_type=f32)
          + jnp.dot(yw_ref[...].astype(bf16), ww_ref[...], preferred_element_type=f32))
    x1_ref[...] = x1
    ms = jnp.mean(x1 * x1, axis=-1, keepdims=True)
    h_ref[...] = x1 * lax.rsqrt(ms + EPS) * g_ref[...]


def _outproj(y_ret, y_rwkv, x, w_ret_bf, w_rwkv_bf, g, tm):
    T, D = x.shape
    row = lambda shape: pl.BlockSpec(shape, lambda i: (i, 0))
    full = lambda shape: pl.BlockSpec(shape, lambda i: (0, 0))
    return pl.pallas_call(
        _outproj_kernel,
        grid=(T // tm,),
        in_specs=[row((tm, RET_V_W)), row((tm, RWKV_W)), row((tm, D)),
                  full(w_ret_bf.shape), full(w_rwkv_bf.shape), full((1, D))],
        out_specs=[row((tm, D)), row((tm, D))],
        out_shape=[jax.ShapeDtypeStruct((T, D), f32), jax.ShapeDtypeStruct((T, D), f32)],
        compiler_params=pltpu.CompilerParams(
            dimension_semantics=("parallel",), vmem_limit_bytes=VMEM_LIMIT),
        name="outproj",
    )(y_ret, y_rwkv, x, w_ret_bf, w_rwkv_bf, g)


def _topk_rows(s_ref, n, vals_ref, idx_ref, lanes):
    iota = lax.broadcasted_iota(jnp.int32, (n, LANES), 0).astype(f32)
    s = s_ref[0:n, lanes]
    for it in range(PEER_TOPK):
        m = jnp.max(s, axis=0, keepdims=True)
        idx = jnp.min(jnp.where(s == m, iota, float(n)), axis=0, keepdims=True)
        vals_ref[it:it + 1, lanes] = m
        idx_ref[it:it + 1, lanes] = idx
        s = jnp.where(iota == idx, -jnp.inf, s)


def _peer_topk_kernel(h_ref, wq_ref, sk_ref, e_ref, g_ref, s_scr, v1, i1, v2, i2, tv, tp, *, tm):
    q = jnp.dot(h_ref[...].astype(bf16), wq_ref[...], preferred_element_type=f32)
    K = PEER_TOPK
    for lt in range(tm // LANES):
        lanes = slice(lt * LANES, (lt + 1) * LANES)
        qt = q[lanes, :]
        for half, (vr, ir) in enumerate(((v1, i1), (v2, i2))):
            qh = qt[:, half * PEER_HALF:(half + 1) * PEER_HALF]
            s_scr[0:N_KEYS, lanes] = _dot_nt(sk_ref[0, half], qh)
            _topk_rows(s_scr, N_KEYS, vr, ir, lanes)
        for i in range(K):
            s_scr[i * K:(i + 1) * K, lanes] = v1[i:i + 1, lanes] + v2[:, lanes]
        _topk_rows(s_scr, K * K, tv, tp, lanes)
        pos = tp[:, lanes]
        pi = jnp.floor(pos * (1.0 / K))
        pj = pos - pi * K
        e1 = jnp.zeros((K, LANES), f32)
        e2 = jnp.zeros((K, LANES), f32)
        for i in range(K):
            e1 = e1 + jnp.where(pi == float(i), i1[i:i + 1, lanes], 0.0)
            e2 = e2 + jnp.where(pj == float(i), i2[i:i + 1, lanes], 0.0)
        e_ref[:, lanes] = (e1 * float(N_KEYS) + e2).astype(jnp.int32)
        top = tv[:, lanes]
        ex = jnp.exp(top - top[0:1, :])
        g_ref[:, lanes] = ex / jnp.sum(ex, axis=0, keepdims=True)


def _peer_topk(h2, wq_bf, sk_bf, tm):
    T, D = h2.shape
    K = PEER_TOPK
    return pl.pallas_call(
        functools.partial(_peer_topk_kernel, tm=tm),
        grid=(T // tm, PEER_HEADS),
        in_specs=[
            pl.BlockSpec((tm, D), lambda i, h: (i, 0)),
            pl.BlockSpec((D, PEER_DK), lambda i, h: (0, h)),
            pl.BlockSpec((1, 2, N_KEYS, PEER_HALF), lambda i, h: (h, 0, 0, 0)),
        ],
        out_specs=[
            pl.BlockSpec((K, tm), lambda i, h: (h, i)),
            pl.BlockSpec((K, tm), lambda i, h: (h, i)),
        ],
        out_shape=[jax.ShapeDtypeStruct((PEER_SLOTS, T), jnp.int32),
                   jax.ShapeDtypeStruct((PEER_SLOTS, T), f32)],
        scratch_shapes=[pltpu.VMEM((K * K, tm), f32)] + [pltpu.VMEM((K, tm), f32)] * 6,
        compiler_params=pltpu.CompilerParams(
            dimension_semantics=("parallel", "arbitrary"), vmem_limit_bytes=VMEM_LIMIT),
        name="peer_topk",
    )(h2, wq_bf, sk_bf)


PEER_NBUF = 4
PEER_GROUP = 8


def _peer_mix_kernel(idx_ref, gate_ref, h_ref, x1_ref, fg_ref, u_hbm, v_hbm, o_ref, ubuf, vbuf, usem, vsem,
                     *, tb):
    S = PEER_SLOTS

    def issue(t, slot):
        for e in range(S):
            row = idx_ref[t, e]
            pltpu.make_async_copy(u_hbm.at[pl.ds(row, 1), :], ubuf.at[slot, pl.ds(e, 1), :],
                                  usem.at[slot]).start()
            pltpu.make_async_copy(v_hbm.at[pl.ds(row, 1), :], vbuf.at[slot, pl.ds(e, 1), :],
                                  vsem.at[slot]).start()

    def wait(slot):
        pltpu.make_async_copy(u_hbm.at[pl.ds(0, S), :], ubuf.at[slot], usem.at[slot]).wait()
        pltpu.make_async_copy(v_hbm.at[pl.ds(0, S), :], vbuf.at[slot], vsem.at[slot]).wait()

    for t in range(PEER_NBUF):
        issue(t, t)

    fg = fg_ref[...]

    def group(gi, carry):
        t0 = pl.multiple_of(gi * PEER_GROUP, PEER_GROUP)
        gcols = jnp.transpose(gate_ref[pl.ds(t0, PEER_GROUP), :])
        for j in range(PEER_GROUP):
            t = t0 + j
            slot = j % PEER_NBUF
            wait(slot)
            xt = h_ref[pl.ds(t, 1), :]
            a = jnp.sum(ubuf[slot] * xt, axis=1, keepdims=True)
            act = 0.5 * a * (1.0 + lax.erf(a * (2.0 ** -0.5)))
            cw = gcols[:, j:j + 1] * act
            out = jnp.sum(vbuf[slot] * cw, axis=0, keepdims=True)
            zt = x1_ref[pl.ds(t, 1), :] + out
            ms = jnp.mean(zt * zt, axis=-1, keepdims=True)
            o_ref[pl.ds(t, 1), :] = zt * lax.rsqrt(ms + EPS) * fg

            @pl.when(t + PEER_NBUF < tb)
            def _():
                issue(t + PEER_NBUF, slot)
        return carry

    lax.fori_loop(0, tb // PEER_GROUP, group, 0)


def _peer_mix(idx, gates, h2, x1, fg, expert_u, expert_v, tb):
    T, D = h2.shape
    S = PEER_SLOTS
    assert PEER_GROUP % PEER_NBUF == 0 and tb % PEER_GROUP == 0 and tb >= PEER_NBUF
    row = lambda shape, **kw: pl.BlockSpec(shape, lambda i: (i, 0), **kw)
    return pl.pallas_call(
        functools.partial(_peer_mix_kernel, tb=tb),
        grid=(T // tb,),
        in_specs=[
            row((tb, S), memory_space=pltpu.SMEM),
            row((tb, S)),
            row((tb, D)),
            row((tb, D)),
            pl.BlockSpec((1, D), lambda i: (0, 0)),
            pl.BlockSpec(memory_space=pl.ANY),
            pl.BlockSpec(memory_space=pl.ANY),
        ],
        out_specs=row((tb, D)),
        out_shape=jax.ShapeDtypeStruct((T, D), f32),
        scratch_shapes=[
            pltpu.VMEM((PEER_NBUF, S, D), f32),
            pltpu.VMEM((PEER_NBUF, S, D), f32),
            pltpu.SemaphoreType.DMA((PEER_NBUF,)),
            pltpu.SemaphoreType.DMA((PEER_NBUF,)),
        ],
        compiler_params=pltpu.CompilerParams(
            dimension_semantics=("arbitrary",), vmem_limit_bytes=VMEM_LIMIT),
        name="peer_mix",
    )(idx, gates, h2, x1, fg, expert_u, expert_v)


def _rwkv_consts(mu, w0, w2, a0, a2, g2, k_k, k_a, r_k, lnx_g, lnx_b):
    pad = RWKV_PROJ_PAD - RWKV_PROJ
    mu_p = jnp.pad(mu, (0, pad)).reshape(1, RWKV_PROJ_PAD)
    w2_p = jnp.concatenate([w2, jnp.zeros((A_LORA, RWKV_W), f32)], 0)
    a2_p = jnp.concatenate([jnp.zeros((W_LORA, RWKV_W), f32), a2], 0)
    g2_p = jnp.concatenate([g2, jnp.zeros((2 * LANES - G_LORA, RWKV_W), f32)], 0)
    hd = np.arange(2 * RWKV_HEAD) // RWKV_HEAD
    bd = jnp.asarray((hd[:, None] == hd[None, :]).astype(np.float32), dtype=bf16)
    r1 = lambda x: x.reshape(1, RWKV_W)
    return (mu_p, r1(w0), r1(a0), w2_p, a2_p, g2_p, r1(k_k), r1(k_a), r1(r_k), r1(lnx_g), r1(lnx_b), bd)


def kernel(x_prompt, x_sample, state_ret, state_rwkv, state_rwkv_shift, norm1_g, w_in, ret_gain, rwkv_mu,
           rwkv_w0, rwkv_w2, rwkv_a0, rwkv_a2, rwkv_g2, rwkv_k_k, rwkv_k_a, rwkv_r_k, lnx_g, lnx_b, w_out,
           norm2_g, peer_w_query, peer_sub_keys, peer_u, peer_v, final_g):
    Bp, Lp, D = x_prompt.shape
    Bs, Ls, _ = x_sample.shape
    Tp, Ts = Bp * Lp, Bs * Ls
    T = Tp + Ts
    past_len = 2048
    x = jnp.concatenate([x_prompt.reshape(Tp, D), x_sample.reshape(Ts, D)], 0)

    l = 0
    w_in_l = w_in[l]
    w_ret = w_in_l[:, :RET_PROJ].astype(bf16)
    w_rwkv = jnp.pad(w_in_l[:, RET_PROJ:], ((0, 0), (0, RWKV_PROJ_PAD - RWKV_PROJ))).astype(bf16)
    g1 = norm1_g[l].reshape(1, D)
    p_ret = _inproj(x, g1, w_ret, 1024, 512)
    p_rwkv = _inproj(x, g1, w_rwkv, 1024, 512)

    gain = ret_gain[l].reshape(1, RET_V_W)
    zero_ret = jnp.zeros((Bp, RET_HEADS, RET_DK, RET_DV), f32)
    yr_p, sr_p = _retention(p_ret, 0, Bp, Lp, 256, 0, zero_ret, gain)
    yr_s, sr_s = _retention(p_ret, Tp, Bs, Ls, Ls, past_len, state_ret[l], gain)

    consts = _rwkv_consts(rwkv_mu[l], rwkv_w0[l], rwkv_w2[l], rwkv_a0[l], rwkv_a2[l], rwkv_g2[l], rwkv_k_k[l],
                          rwkv_k_a[l], rwkv_r_k[l].reshape(-1), lnx_g[l], lnx_b[l])
    pad = RWKV_PROJ_PAD - RWKV_PROJ
    sh_p = jnp.zeros((Bp, 1, RWKV_PROJ_PAD), f32)
    sh_s = jnp.pad(state_rwkv_shift[l], ((0, 0), (0, 0), (0, pad)))
    zero_rwkv = jnp.zeros((Bp, RWKV_HEADS, RWKV_HEAD, RWKV_HEAD), f32)
    yw_p, sw_p = _rwkv(p_rwkv, 0, Bp, Lp, sh_p, zero_rwkv, consts)
    yw_s, sw_s = _rwkv(p_rwkv, Tp, Bs, Ls, sh_s, state_rwkv[l], consts)

    y_ret = jnp.concatenate([yr_p, yr_s], 0)
    y_rwkv = jnp.concatenate([yw_p, yw_s], 0)
    w_out_l = w_out[l].astype(bf16)
    x1, h2 = _outproj(y_ret, y_rwkv, x, w_out_l[:RET_V_W], w_out_l[RET_V_W:], norm2_g[l].reshape(1, D), 512)

    e_t, g_t = _peer_topk(h2, peer_w_query[l].astype(bf16), peer_sub_keys[l].astype(bf16), 256)
    y = _peer_mix(e_t.T, g_t.T, h2, x1, final_g.reshape(1, D), peer_u[l], peer_v[l], 128)

    y_prompt = y[:Tp].reshape(Bp, Lp, D)
    y_sample = y[Tp:].reshape(Bs, Ls, D)
    shift_p = p_rwkv[:Tp].reshape(Bp, Lp, RWKV_PROJ_PAD)[:, -1:, :RWKV_PROJ]
    shift_s = p_rwkv[Tp:].reshape(Bs, Ls, RWKV_PROJ_PAD)[:, -1:, :RWKV_PROJ]
    return (y_prompt, y_sample, sr_p[None], sw_p[None], shift_p[None], sr_s[None], sw_s[None], shift_s[None])
```

```python
import functools
import math

import numpy as np
import jax
import jax.numpy as jnp
from jax import lax
from jax.experimental import pallas as pl
from jax.experimental.pallas import tpu as pltpu

f32 = jnp.float32
bf16 = jnp.bfloat16

D_MODEL = 2048
CHUNK = 64
EPS = 1e-6
RET_HEADS = 4
RET_DK = 128
RET_DV = 256
RET_QK_W = RET_HEADS * RET_DK
RET_V_W = RET_HEADS * RET_DV
RET_PROJ = 2 * RET_QK_W + 2 * RET_V_W
ROPE_BASE = 10000.0
RWKV_HEAD = 64
RWKV_W = D_MODEL // 2
RWKV_HEADS = RWKV_W // RWKV_HEAD
RWKV_PAIRS = RWKV_HEADS // 2
W_LORA = 64
A_LORA = 64
G_LORA = 160
RWKV_PROJ = 3 * RWKV_W + W_LORA + A_LORA + G_LORA
RWKV_PROJ_PAD = 3584
RWKV_GN_EPS = 64e-5
PEER_HEADS = 8
N_KEYS = 128
PEER_DK = 256
PEER_HALF = PEER_DK // 2
PEER_TOPK = 16
PEER_SLOTS = PEER_HEADS * PEER_TOPK

LANES = 128
VMEM_LIMIT = 56 * 1024 * 1024


def _dot(a, b):
    return jnp.dot(a.astype(bf16), b.astype(bf16), preferred_element_type=f32)


def _dot_nt(a, b):
    return lax.dot_general(a.astype(bf16), b.astype(bf16), (((1,), (1,)), ((), ())),
                           preferred_element_type=f32)


def _dot_tn(a, b):
    return lax.dot_general(a.astype(bf16), b.astype(bf16), (((0,), (0,)), ((), ())),
                           preferred_element_type=f32)


def _split2(x):
    hi = x.astype(bf16)
    lo = (x - hi.astype(f32)).astype(bf16)
    return hi, lo


def _split3(x):
    hi = x.astype(bf16)
    r1 = x - hi.astype(f32)
    mid = r1.astype(bf16)
    lo = (r1 - mid.astype(f32)).astype(bf16)
    return hi, mid, lo


def _dot_exact_rhs(a, b_bf):
    hi, mid, lo = _split3(a)
    return (jnp.dot(hi, b_bf, preferred_element_type=f32)
            + jnp.dot(mid, b_bf, preferred_element_type=f32)
            + jnp.dot(lo, b_bf, preferred_element_type=f32))


def _dot_exact_lhs(a_bf, b):
    hi, mid, lo = _split3(b)
    return (jnp.dot(a_bf, hi, preferred_element_type=f32)
            + jnp.dot(a_bf, mid, preferred_element_type=f32)
            + jnp.dot(a_bf, lo, preferred_element_type=f32))


def _dot_hi(a, b):
    ah, al = _split2(a)
    bh, bl = _split2(b)
    return (jnp.dot(ah, bh, preferred_element_type=f32)
            + jnp.dot(ah, bl, preferred_element_type=f32)
            + jnp.dot(al, bh, preferred_element_type=f32))


def _inproj_kernel(x_ref, g_ref, w_ref, o_ref, h_ref):
    @pl.when(pl.program_id(1) == 0)
    def _():
        x = x_ref[...]
        ms = jnp.mean(x * x, axis=-1, keepdims=True)
        h_ref[...] = (x * lax.rsqrt(ms + EPS) * g_ref[...]).astype(bf16)

    o_ref[...] = jnp.dot(h_ref[...], w_ref[...], preferred_element_type=f32)


def _inproj(x, g, w_bf, tm, tn):
    T, D = x.shape
    N = w_bf.shape[1]
    return pl.pallas_call(
        _inproj_kernel,
        grid=(T // tm, N // tn),
        in_specs=[
            pl.BlockSpec((tm, D), lambda i, j: (i, 0)),
            pl.BlockSpec((1, D), lambda i, j: (0, 0)),
            pl.BlockSpec((D, tn), lambda i, j: (0, j)),
        ],
        out_specs=pl.BlockSpec((tm, tn), lambda i, j: (i, j)),
        out_shape=jax.ShapeDtypeStruct((T, N), f32),
        scratch_shapes=[pltpu.VMEM((tm, D), bf16)],
        compiler_params=pltpu.CompilerParams(
            dimension_semantics=("parallel", "arbitrary"), vmem_limit_bytes=VMEM_LIMIT),
        name="inproj",
    )(x, g, w_bf)


def _ret_kernel(q_ref, k_ref, v_ref, gt_ref, cs_ref, sn_ref, mask_ref, dec_ref, gain_ref, s0_ref,
                y_ref, sout_ref, s_scr, *, nblk):
    c = pl.program_id(2)

    @pl.when(c == 0)
    def _():
        s_scr[...] = s0_ref[0, 0]

    q = q_ref[...]
    k = k_ref[...]
    cs = cs_ref[...]
    sn = sn_ref[...]
    half = RET_DK // 2
    qr = q * cs + pltpu.roll(q, half, 1) * sn
    kr = (k * cs + pltpu.roll(k, half, 1) * sn) * (RET_DK ** -0.5)
    dec = dec_ref[0]
    qdec = dec[:, 0:1]
    kdec = dec[:, 1:2]
    gblk = dec[0:1, 2:3]
    v = v_ref[...]
    s = s_scr[...]
    sc = _dot_nt(qr, kr) * mask_ref[0]
    o = _dot(sc, v) + _dot(qr * qdec, s)
    s_new = gblk * s + _dot_tn(kr * kdec, v)
    s_scr[...] = s_new
    ms = jnp.mean(o * o, axis=-1, keepdims=True)
    on = o * lax.rsqrt(ms + EPS) * gain_ref[...]
    gt = gt_ref[...]
    y_ref[...] = gt * jax.nn.sigmoid(gt) * on

    @pl.when(c == nblk - 1)
    def _():
        sout_ref[0, 0] = s_new


def _ret_tables(L, cb, pos0, chunked):
    half = RET_DK // 2
    inv = ROPE_BASE ** (-jnp.arange(half, dtype=f32) / half)
    pos = (pos0 + jnp.arange(L, dtype=jnp.int32)).astype(f32)
    ang = pos[:, None] * inv[None, :]
    cos = jnp.cos(ang)
    sin = jnp.sin(ang)
    cs = jnp.concatenate([cos, cos], -1)
    sn = jnp.concatenate([-sin, sin], -1)
    log_gamma = jnp.log1p(-jnp.exp2(-5.0 - jnp.arange(RET_HEADS, dtype=f32)))
    idx = jnp.arange(cb, dtype=f32)
    dist = jnp.abs(idx[:, None] - idx[None, :])
    mask = jnp.exp(log_gamma[:, None, None] * dist[None])
    if chunked:
        ci = jnp.arange(cb) // CHUNK
        mask = jnp.where((ci[None, :] <= ci[:, None])[None], mask, 0.0)
    qdec = jnp.exp(log_gamma[:, None] * (idx[None, :] + 1.0))
    kdec = jnp.exp(log_gamma[:, None] * (cb - 1.0 - idx[None, :]))
    gblk = jnp.broadcast_to(jnp.exp(log_gamma * cb)[:, None], (RET_HEADS, cb))
    dec = jnp.zeros((RET_HEADS, cb, LANES), f32)
    dec = dec.at[:, :, 0].set(qdec).at[:, :, 1].set(kdec).at[:, :, 2].set(gblk)
    return cs, sn, mask, dec


def _retention(p_ret, row0, B, L, cb, pos0, s0, gain):
    nblk = L // cb
    rb0 = row0 // cb
    cs, sn, mask, dec = _ret_tables(L, cb, pos0, chunked=L > CHUNK)
    T = p_ret.shape[0]
    nq = RET_QK_W // RET_DK
    row = lambda b, h, c: rb0 + b * nblk + c
    y, s_out = pl.pallas_call(
        functools.partial(_ret_kernel, nblk=nblk),
        grid=(B, RET_HEADS, nblk),
        in_specs=[
            pl.BlockSpec((cb, RET_DK), lambda b, h, c: (row(b, h, c), h)),
            pl.BlockSpec((cb, RET_DK), lambda b, h, c: (row(b, h, c), nq + h)),
            pl.BlockSpec((cb, RET_DV), lambda b, h, c: (row(b, h, c), nq + h)),
            pl.BlockSpec((cb, RET_DV), lambda b, h, c: (row(b, h, c), 2 * nq + h)),
            pl.BlockSpec((cb, RET_DK), lambda b, h, c: (c, 0)),
            pl.BlockSpec((cb, RET_DK), lambda b, h, c: (c, 0)),
            pl.BlockSpec((1, cb, cb), lambda b, h, c: (h, 0, 0)),
            pl.BlockSpec((1, cb, LANES), lambda b, h, c: (h, 0, 0)),
            pl.BlockSpec((1, RET_DV), lambda b, h, c: (0, h)),
            pl.BlockSpec((1, 1, RET_DK, RET_DV), lambda b, h, c: (b, h, 0, 0)),
        ],
        out_specs=[
            pl.BlockSpec((cb, RET_DV), lambda b, h, c: (b * nblk + c, h)),
            pl.BlockSpec((1, 1, RET_DK, RET_DV), lambda b, h, c: (b, h, 0, 0)),
        ],
        out_shape=[
            jax.ShapeDtypeStruct((B * L, RET_V_W), f32),
            jax.ShapeDtypeStruct((B, RET_HEADS, RET_DK, RET_DV), f32),
        ],
        scratch_shapes=[pltpu.VMEM((RET_DK, RET_DV), f32)],
        compiler_params=pltpu.CompilerParams(
            dimension_semantics=("arbitrary", "arbitrary", "arbitrary"), vmem_limit_bytes=VMEM_LIMIT),
        name=f"retention_{L}",
    )(p_ret, p_ret, p_ret, p_ret, cs, sn, mask, dec, gain, s0)
    return y, s_out


def _rwkv_kernel(p_ref, sh0_ref, s0_ref, mu_ref, w0_ref, a0_ref, w2_ref, a2_ref, g2_ref, kk_ref, ka_ref,
                 rk_ref, lg_ref, lb_ref, bd_ref, y_ref, sout_ref, carry, s_scr, *, nvalid, nchunk):
    C = CHUNK
    c = pl.program_id(1)

    @pl.when(c == 0)
    def _():
        carry[...] = sh0_ref[0]
        s_scr[...] = s0_ref[0]

    if nvalid == C:
        pf = p_ref[...]
    else:
        pf = jnp.concatenate([p_ref[...], jnp.zeros((C - nvalid, RWKV_PROJ_PAD), f32)], axis=0)
    rows = lax.broadcasted_iota(jnp.int32, (C, 1), 0)
    prev = jnp.where(rows == 0, carry[...], pltpu.roll(pf, 1, 0))
    pm = pf + (prev - pf) * mu_ref[...]
    carry[...] = pf[nvalid - 1:nvalid, :]

    o1, o2, o3 = RWKV_W, 2 * RWKV_W, 3 * RWKV_W
    r = pm[:, :o1]
    k = pm[:, o1:o2]
    v = pm[:, o2:o3]
    xwa = pm[:, o3:o3 + W_LORA + A_LORA]
    xg = pm[:, o3 + W_LORA + A_LORA:o3 + W_LORA + A_LORA + 2 * LANES]
    z = w0_ref[...] + _dot_hi(jnp.tanh(xwa), w2_ref[...])
    softplus = jnp.maximum(-z, 0.0) + jnp.log1p(jnp.exp(-jnp.abs(z)))
    ew = jnp.exp(-softplus - 0.5)
    a = jax.nn.sigmoid(a0_ref[...] + _dot_hi(xwa, a2_ref[...]))
    g = _dot_hi(jax.nn.sigmoid(xg), g2_ref[...])
    k2 = k * (1.0 + (a - 1.0) * ka_ref[...])
    kkr = k * kk_ref[...]
    if nvalid != C:
        valid = rows < nvalid
        ew = jnp.where(valid, ew, 0.0)
        k2 = jnp.where(valid, k2, 0.0)
        v = jnp.where(valid, v, 0.0)
        a = jnp.where(valid, a, 0.0)

    ti = lax.broadcasted_iota(jnp.int32, (C, C), 0)
    tj = lax.broadcasted_iota(jnp.int32, (C, C), 1)
    tri = jnp.where(tj <= ti, 1.0, 0.0).astype(bf16)
    cum = _dot_exact_lhs(tri, -ew)
    pt = jnp.exp(cum)
    pprev = jnp.exp(cum + ew)
    pinv = jnp.exp(-cum)
    clast = cum[C - 1:C, :]
    pcinv = jnp.exp(clast - cum)
    pc = jnp.exp(clast)
    rt_all = r * pt
    kt_all = k2 * pinv
    kp_all = k2 * pcinv
    rkk = r * k2 * rk_ref[...]

    lane = lax.broadcasted_iota(jnp.int32, (C, 2 * RWKV_HEAD), 1)
    m0 = lane < RWKV_HEAD
    gi = lax.broadcasted_iota(jnp.int32, (4 * C, 4 * C), 0)
    gj = lax.broadcasted_iota(jnp.int32, (4 * C, 4 * C), 1)
    gil = gi % C
    gjl = gj % C
    gmask = gjl < gil + jnp.where(gi < 2 * C, 0, 1)
    ei = lax.broadcasted_iota(jnp.int32, (2 * C, 2 * C), 0)
    ej = lax.broadcasted_iota(jnp.int32, (2 * C, 2 * C), 1)
    eye = jnp.where(ei == ej, 1.0, 0.0)
    bd = bd_ref[...]

    def stack(x):
        return jnp.concatenate([jnp.where(m0, x, 0.0), jnp.where(m0, 0.0, x)], axis=0)

    def gsum(x):
        hi, lo = _split2(x)
        return jnp.dot(hi, bd, preferred_element_type=f32) + jnp.dot(lo, bd, preferred_element_type=f32)

    nsq = int(math.log2(C))
    for p in range(RWKV_PAIRS):
        ls = slice(p * 2 * RWKV_HEAD, (p + 1) * 2 * RWKV_HEAD)
        kkp = kkr[:, ls]
        nrm = jnp.sqrt(gsum(kkp * kkp))
        kk = kkp / jnp.maximum(nrm, 1e-12)
        ap = a[:, ls]
        bb = kk * ap
        at_s = stack(-kk * pprev[:, ls])
        rt_s = stack(rt_all[:, ls])
        bt_s = stack(bb * pinv[:, ls])
        kt_s = stack(kt_all[:, ls])
        bp_s = stack(bb * pcinv[:, ls])
        kp_s = stack(kp_all[:, ls])
        v_p = v[:, ls]
        v_s = stack(v_p)
        s0 = s_scr[p]

        ar = jnp.concatenate([at_s, rt_s], axis=0)
        bk = jnp.concatenate([bt_s, kt_s], axis=0)
        gm = jnp.where(gmask, _dot_nt(ar, bk), 0.0)
        nd = gm[:2 * C, :2 * C]
        aak = gm[:2 * C, 2 * C:]
        rbk = gm[2 * C:, :]
        arh = _dot_nt(ar, s0)
        rhs = arh[:2 * C] + _dot(aak, v_s)
        tinv = eye + nd
        npow = nd
        for i in range(1, nsq):
            npow = _dot(npow, npow)
            tinv = tinv + _dot(npow, tinv)
        u_s = _dot(tinv, rhs)
        uv = jnp.concatenate([u_s, v_s], axis=0)
        y_s = arh[2 * C:] + _dot(rbk, uv)
        yp = y_s[:C] + y_s[C:]
        s_new = s0 * pc[:, ls] + _dot_tn(uv, jnp.concatenate([bp_s, kp_s], axis=0))
        s_scr[p] = s_new

        mean = gsum(yp) * (1.0 / RWKV_HEAD)
        yc = yp - mean
        var = gsum(yc * yc) * (1.0 / RWKV_HEAD)
        yn = yc * lax.rsqrt(var + RWKV_GN_EPS) * lg_ref[:, ls] + lb_ref[:, ls]
        bonus = gsum(rkk[:, ls]) * v_p
        yo = (yn + bonus) * g[:, ls]
        y_ref[:, ls] = yo[:nvalid]

    @pl.when(c == nchunk - 1)
    def _():
        sout_ref[0] = s_scr[...]


def _pair_states(s):
    B = s.shape[0]
    sp = s.reshape(B, RWKV_PAIRS, 2, RWKV_HEAD, RWKV_HEAD)
    z = jnp.zeros_like(sp[:, :, 0])
    top = jnp.concatenate([sp[:, :, 0], z], -1)
    bot = jnp.concatenate([z, sp[:, :, 1]], -1)
    return jnp.concatenate([top, bot], -2)


def _unpair_states(sp):
    B = sp.shape[0]
    h = RWKV_HEAD
    return jnp.stack([sp[:, :, :h, :h], sp[:, :, h:, h:]], 2).reshape(B, RWKV_HEADS, h, h)


def _rwkv(p_rwkv, row0, B, L, sh0, s0, consts):
    nvalid = min(L, CHUNK)
    nchunk = L // nvalid
    rb0 = row0 // nvalid
    full = lambda shape: pl.BlockSpec(shape, lambda b, c: (0,) * len(shape))
    in_specs = [
        pl.BlockSpec((nvalid, RWKV_PROJ_PAD), lambda b, c: (rb0 + b * nchunk + c, 0)),
        pl.BlockSpec((1, 1, RWKV_PROJ_PAD), lambda b, c: (b, 0, 0)),
        pl.BlockSpec((1, RWKV_PAIRS, 2 * RWKV_HEAD, 2 * RWKV_HEAD), lambda b, c: (b, 0, 0, 0)),
    ] + [full(x.shape) for x in consts]
    y, s_out = pl.pallas_call(
        functools.partial(_rwkv_kernel, nvalid=nvalid, nchunk=nchunk),
        grid=(B, nchunk),
        in_specs=in_specs,
        out_specs=[
            pl.BlockSpec((nvalid, RWKV_W), lambda b, c: (b * nchunk + c, 0)),
            pl.BlockSpec((1, RWKV_PAIRS, 2 * RWKV_HEAD, 2 * RWKV_HEAD), lambda b, c: (b, 0, 0, 0)),
        ],
        out_shape=[
            jax.ShapeDtypeStruct((B * L, RWKV_W), f32),
            jax.ShapeDtypeStruct((B, RWKV_PAIRS, 2 * RWKV_HEAD, 2 * RWKV_HEAD), f32),
        ],
        scratch_shapes=[
            pltpu.VMEM((1, RWKV_PROJ_PAD), f32),
            pltpu.VMEM((RWKV_PAIRS, 2 * RWKV_HEAD, 2 * RWKV_HEAD), f32),
        ],
        compiler_params=pltpu.CompilerParams(
            dimension_semantics=("arbitrary", "arbitrary"), vmem_limit_bytes=VMEM_LIMIT),
        name=f"rwkv7_{L}",
    )(p_rwkv, sh0, _pair_states(s0), *consts)
    return y, _unpair_states(s_out)


def _outproj_kernel(yr_ref, yw_ref, x_ref, wr_ref, ww_ref, g_ref, x1_ref, h_ref):
    x1 = (x_ref[...] + jnp.dot(yr_ref[...].astype(bf16), wr_ref[...], preferred_element_type=f32)
          + jnp.dot(yw_ref[...].astype(bf16), ww_ref[...], preferred_element_type=f32))
    x1_ref[...] = x1
    ms = jnp.mean(x1 * x1, axis=-1, keepdims=True)
    h_ref[...] = x1 * lax.rsqrt(ms + EPS) * g_ref[...]


def _outproj(y_ret, y_rwkv, x, w_ret_bf, w_rwkv_bf, g, tm):
    T, D = x.shape
    row = lambda shape: pl.BlockSpec(shape, lambda i: (i, 0))
    full = lambda shape: pl.BlockSpec(shape, lambda i: (0, 0))
    return pl.pallas_call(
        _outproj_kernel,
        grid=(T // tm,),
        in_specs=[row((tm, RET_V_W)), row((tm, RWKV_W)), row((tm, D)),
                  full(w_ret_bf.shape), full(w_rwkv_bf.shape), full((1, D))],
        out_specs=[row((tm, D)), row((tm, D))],
        out_shape=[jax.ShapeDtypeStruct((T, D), f32), jax.ShapeDtypeStruct((T, D), f32)],
        compiler_params=pltpu.CompilerParams(
            dimension_semantics=("parallel",), vmem_limit_bytes=VMEM_LIMIT),
        name="outproj",
    )(y_ret, y_rwkv, x, w_ret_bf, w_rwkv_bf, g)


def _topk_rows(s_ref, n, vals_ref, idx_ref, lanes):
    iota = lax.broadcasted_iota(jnp.int32, (n, LANES), 0).astype(f32)
    s = s_ref[0:n, lanes]
    for it in range(PEER_TOPK):
        m = jnp.max(s, axis=0, keepdims=True)
        idx = jnp.min(jnp.where(s == m, iota, float(n)), axis=0, keepdims=True)
        vals_ref[it:it + 1, lanes] = m
        idx_ref[it:it + 1, lanes] = idx
        s = jnp.where(iota == idx, -jnp.inf, s)


def _peer_topk_kernel(h_ref, wq_ref, sk_ref, e_ref, g_ref, s_scr, v1, i1, v2, i2, tv, tp, *, tm):
    q = jnp.dot(h_ref[...].astype(bf16), wq_ref[...], preferred_element_type=f32)
    K = PEER_TOPK
    for lt in range(tm // LANES):
        lanes = slice(lt * LANES, (lt + 1) * LANES)
        qt = q[lanes, :]
        for half, (vr, ir) in enumerate(((v1, i1), (v2, i2))):
            qh = qt[:, half * PEER_HALF:(half + 1) * PEER_HALF]
            s_scr[0:N_KEYS, lanes] = _dot_nt(sk_ref[0, half], qh)
            _topk_rows(s_scr, N_KEYS, vr, ir, lanes)
        for i in range(K):
            s_scr[i * K:(i + 1) * K, lanes] = v1[i:i + 1, lanes] + v2[:, lanes]
        _topk_rows(s_scr, K * K, tv, tp, lanes)
        pos = tp[:, lanes]
        pi = jnp.floor(pos * (1.0 / K))
        pj = pos - pi * K
        e1 = jnp.zeros((K, LANES), f32)
        e2 = jnp.zeros((K, LANES), f32)
        for i in range(K):
            e1 = e1 + jnp.where(pi == float(i), i1[i:i + 1, lanes], 0.0)
            e2 = e2 + jnp.where(pj == float(i), i2[i:i + 1, lanes], 0.0)
        e_ref[:, lanes] = (e1 * float(N_KEYS) + e2).astype(jnp.int32)
        top = tv[:, lanes]
        ex = jnp.exp(top - top[0:1, :])
        g_ref[:, lanes] = ex / jnp.sum(ex, axis=0, keepdims=True)


def _peer_topk(h2, wq_bf, sk_bf, tm):
    T, D = h2.shape
    K = PEER_TOPK
    return pl.pallas_call(
        functools.partial(_peer_topk_kernel, tm=tm),
        grid=(T // tm, PEER_HEADS),
        in_specs=[
            pl.BlockSpec((tm, D), lambda i, h: (i, 0)),
            pl.BlockSpec((D, PEER_DK), lambda i, h: (0, h)),
            pl.BlockSpec((1, 2, N_KEYS, PEER_HALF), lambda i, h: (h, 0, 0, 0)),
        ],
        out_specs=[
            pl.BlockSpec((K, tm), lambda i, h: (h, i)),
            pl.BlockSpec((K, tm), lambda i, h: (h, i)),
        ],
        out_shape=[jax.ShapeDtypeStruct((PEER_SLOTS, T), jnp.int32),
                   jax.ShapeDtypeStruct((PEER_SLOTS, T), f32)],
        scratch_shapes=[pltpu.VMEM((K * K, tm), f32)] + [pltpu.VMEM((K, tm), f32)] * 6,
        compiler_params=pltpu.CompilerParams(
            dimension_semantics=("parallel", "arbitrary"), vmem_limit_bytes=VMEM_LIMIT),
        name="peer_topk",
    )(h2, wq_bf, sk_bf)


PEER_NBUF = 8
PEER_GROUP = 8
DS = D_MODEL // LANES
EROWS = 2 * DS


def _peer_mix_kernel(idx_ref, gate_ref, h_ref, x1_ref, fg_ref, uv_hbm, o_ref, *scratch, tb):
    bufs = scratch[:PEER_NBUF]
    out_scr, z_scr, sem = scratch[PEER_NBUF:]
    S = PEER_SLOTS
    G = PEER_GROUP
    ngroups = tb // G

    def issue(t, slot):
        for e in range(S):
            pltpu.make_async_copy(uv_hbm.at[idx_ref[t, e]], bufs[slot].at[:, e, :], sem.at[slot]).start()

    def wait(slot):
        pltpu.make_async_copy(bufs[slot], bufs[slot], sem.at[slot]).wait()

    for t in range(PEER_NBUF):
        issue(t, t)

    fg = fg_ref[...]

    def group(t0, first, last):
        gcols = jnp.transpose(gate_ref[pl.ds(t0, G), :])
        for j in range(G):
            t = t0 + j
            wait(j)
            if j == 0:
                if not first:
                    issue(t + G - 1, G - 1)
            elif not last:
                issue(t + G - 1, j - 1)
            buf = bufs[j]
            x3 = h_ref[t]
            acc = buf[0] * x3[0:1, :]
            for s in range(1, DS):
                acc = acc + buf[s] * x3[s:s + 1, :]
            a = jnp.sum(acc, axis=1, keepdims=True)
            act = 0.5 * a * (1.0 + lax.erf(a * (2.0 ** -0.5)))
            cw = jnp.broadcast_to(gcols[:, j:j + 1] * act, (S, LANES))
            for s in range(DS):
                out_scr[s:s + 1, :] = jnp.sum(buf[DS + s] * cw, axis=0, keepdims=True)
            z_scr[j] = x1_ref[t] + out_scr[...]
        for j in range(G):
            z3 = z_scr[j]
            ms = jnp.sum(jnp.sum(z3 * z3, axis=1, keepdims=True), axis=0, keepdims=True) * (1.0 / D_MODEL)
            o_ref[t0 + j] = z3 * lax.rsqrt(ms + EPS) * fg

    group(0, True, False)

    def mid(gi, carry):
        group(pl.multiple_of(gi * G, G), False, False)
        return carry

    lax.fori_loop(1, ngroups - 1, mid, 0)
    group((ngroups - 1) * G, False, True)


def _peer_mix(idx, gates, h2, x1, fg, expert_u, expert_v, tb):
    T, D = h2.shape
    S = PEER_SLOTS
    E = expert_u.shape[0]
    assert PEER_GROUP == PEER_NBUF and tb % PEER_GROUP == 0 and tb >= 2 * PEER_GROUP
    uv = jnp.concatenate([expert_u.reshape(E, DS, LANES), expert_v.reshape(E, DS, LANES)], 1)
    row3 = pl.BlockSpec((tb, DS, LANES), lambda i: (i, 0, 0))
    y3 = pl.pallas_call(
        functools.partial(_peer_mix_kernel, tb=tb),
        grid=(T // tb,),
        in_specs=[
            pl.BlockSpec((tb, S), lambda i: (i, 0), memory_space=pltpu.SMEM),
            pl.BlockSpec((tb, S), lambda i: (i, 0)),
            row3,
            row3,
            pl.BlockSpec((DS, LANES), lambda i: (0, 0)),
            pl.BlockSpec(memory_space=pl.ANY),
        ],
        out_specs=row3,
        out_shape=jax.ShapeDtypeStruct((T, DS, LANES), f32),
        scratch_shapes=[pltpu.VMEM((EROWS, S, LANES), f32)] * PEER_NBUF + [
            pltpu.VMEM((DS, LANES), f32),
            pltpu.VMEM((PEER_GROUP, DS, LANES), f32),
            pltpu.SemaphoreType.DMA((PEER_NBUF,)),
        ],
        compiler_params=pltpu.CompilerParams(
            dimension_semantics=("arbitrary",), vmem_limit_bytes=VMEM_LIMIT),
        name="peer_mix",
    )(idx, gates, h2.reshape(T, DS, LANES), x1.reshape(T, DS, LANES), fg.reshape(DS, LANES), uv)
    return y3.reshape(T, D)


def _rwkv_consts(mu, w0, w2, a0, a2, g2, k_k, k_a, r_k, lnx_g, lnx_b):
    pad = RWKV_PROJ_PAD - RWKV_PROJ
    mu_p = jnp.pad(mu, (0, pad)).reshape(1, RWKV_PROJ_PAD)
    w2_p = jnp.concatenate([w2, jnp.zeros((A_LORA, RWKV_W), f32)], 0)
    a2_p = jnp.concatenate([jnp.zeros((W_LORA, RWKV_W), f32), a2], 0)
    g2_p = jnp.concatenate([g2, jnp.zeros((2 * LANES - G_LORA, RWKV_W), f32)], 0)
    hd = np.arange(2 * RWKV_HEAD) // RWKV_HEAD
    bd = jnp.asarray((hd[:, None] == hd[None, :]).astype(np.float32), dtype=bf16)
    r1 = lambda x: x.reshape(1, RWKV_W)
    return (mu_p, r1(w0), r1(a0), w2_p, a2_p, g2_p, r1(k_k), r1(k_a), r1(r_k), r1(lnx_g), r1(lnx_b), bd)


def kernel(x_prompt, x_sample, state_ret, state_rwkv, state_rwkv_shift, norm1_g, w_in, ret_gain, rwkv_mu,
           rwkv_w0, rwkv_w2, rwkv_a0, rwkv_a2, rwkv_g2, rwkv_k_k, rwkv_k_a, rwkv_r_k, lnx_g, lnx_b, w_out,
           norm2_g, peer_w_query, peer_sub_keys, peer_u, peer_v, final_g):
    Bp, Lp, D = x_prompt.shape
    Bs, Ls, _ = x_sample.shape
    Tp, Ts = Bp * Lp, Bs * Ls
    T = Tp + Ts
    past_len = 2048
    x = jnp.concatenate([x_prompt.reshape(Tp, D), x_sample.reshape(Ts, D)], 0)

    l = 0
    w_in_l = w_in[l]
    w_ret = w_in_l[:, :RET_PROJ].astype(bf16)
    w_rwkv = jnp.pad(w_in_l[:, RET_PROJ:], ((0, 0), (0, RWKV_PROJ_PAD - RWKV_PROJ))).astype(bf16)
    g1 = norm1_g[l].reshape(1, D)
    p_ret = _inproj(x, g1, w_ret, 1024, 512)
    p_rwkv = _inproj(x, g1, w_rwkv, 1024, 512)

    gain = ret_gain[l].reshape(1, RET_V_W)
    zero_ret = jnp.zeros((Bp, RET_HEADS, RET_DK, RET_DV), f32)
    yr_p, sr_p = _retention(p_ret, 0, Bp, Lp, 256, 0, zero_ret, gain)
    yr_s, sr_s = _retention(p_ret, Tp, Bs, Ls, Ls, past_len, state_ret[l], gain)

    consts = _rwkv_consts(rwkv_mu[l], rwkv_w0[l], rwkv_w2[l], rwkv_a0[l], rwkv_a2[l], rwkv_g2[l], rwkv_k_k[l],
                          rwkv_k_a[l], rwkv_r_k[l].reshape(-1), lnx_g[l], lnx_b[l])
    pad = RWKV_PROJ_PAD - RWKV_PROJ
    sh_p = jnp.zeros((Bp, 1, RWKV_PROJ_PAD), f32)
    sh_s = jnp.pad(state_rwkv_shift[l], ((0, 0), (0, 0), (0, pad)))
    zero_rwkv = jnp.zeros((Bp, RWKV_HEADS, RWKV_HEAD, RWKV_HEAD), f32)
    yw_p, sw_p = _rwkv(p_rwkv, 0, Bp, Lp, sh_p, zero_rwkv, consts)
    yw_s, sw_s = _rwkv(p_rwkv, Tp, Bs, Ls, sh_s, state_rwkv[l], consts)

    y_ret = jnp.concatenate([yr_p, yr_s], 0)
    y_rwkv = jnp.concatenate([yw_p, yw_s], 0)
    w_out_l = w_out[l].astype(bf16)
    x1, h2 = _outproj(y_ret, y_rwkv, x, w_out_l[:RET_V_W], w_out_l[RET_V_W:], norm2_g[l].reshape(1, D), 512)

    e_t, g_t = _peer_topk(h2, peer_w_query[l].astype(bf16), peer_sub_keys[l].astype(bf16), 256)
    y = _peer_mix(e_t.T, g_t.T, h2, x1, final_g, peer_u[l], peer_v[l], 256)

    y_prompt = y[:Tp].reshape(Bp, Lp, D)
    y_sample = y[Tp:].reshape(Bs, Ls, D)
    shift_p = p_rwkv[:Tp].reshape(Bp, Lp, RWKV_PROJ_PAD)[:, -1:, :RWKV_PROJ]
    shift_s = p_rwkv[Tp:].reshape(Bs, Ls, RWKV_PROJ_PAD)[:, -1:, :RWKV_PROJ]
    return (y_prompt, y_sample, sr_p[None], sw_p[None], shift_p[None], sr_s[None], sw_s[None], shift_s[None])
```

```python
import functools
import math

import numpy as np
import jax
import jax.numpy as jnp
from jax import lax
from jax.experimental import pallas as pl
from jax.experimental.pallas import tpu as pltpu

f32 = jnp.float32
bf16 = jnp.bfloat16

D_MODEL = 2048
CHUNK = 64
EPS = 1e-6
RET_HEADS = 4
RET_DK = 128
RET_DV = 256
RET_QK_W = RET_HEADS * RET_DK
RET_V_W = RET_HEADS * RET_DV
RET_PROJ = 2 * RET_QK_W + 2 * RET_V_W
ROPE_BASE = 10000.0
RWKV_HEAD = 64
RWKV_W = D_MODEL // 2
RWKV_HEADS = RWKV_W // RWKV_HEAD
RWKV_PAIRS = RWKV_HEADS // 2
W_LORA = 64
A_LORA = 64
G_LORA = 160
RWKV_PROJ = 3 * RWKV_W + W_LORA + A_LORA + G_LORA
RWKV_PROJ_PAD = 3584
RWKV_GN_EPS = 64e-5
PEER_HEADS = 8
N_KEYS = 128
PEER_DK = 256
PEER_HALF = PEER_DK // 2
PEER_TOPK = 16
PEER_SLOTS = PEER_HEADS * PEER_TOPK

LANES = 128
VMEM_LIMIT = 56 * 1024 * 1024


def _dot(a, b):
    return jnp.dot(a.astype(bf16), b.astype(bf16), preferred_element_type=f32)


def _dot_nt(a, b):
    return lax.dot_general(a.astype(bf16), b.astype(bf16), (((1,), (1,)), ((), ())),
                           preferred_element_type=f32)


def _dot_tn(a, b):
    return lax.dot_general(a.astype(bf16), b.astype(bf16), (((0,), (0,)), ((), ())),
                           preferred_element_type=f32)


def _split2(x):
    hi = x.astype(bf16)
    lo = (x - hi.astype(f32)).astype(bf16)
    return hi, lo


def _split3(x):
    hi = x.astype(bf16)
    r1 = x - hi.astype(f32)
    mid = r1.astype(bf16)
    lo = (r1 - mid.astype(f32)).astype(bf16)
    return hi, mid, lo


def _dot_exact_rhs(a, b_bf):
    hi, mid, lo = _split3(a)
    return (jnp.dot(hi, b_bf, preferred_element_type=f32)
            + jnp.dot(mid, b_bf, preferred_element_type=f32)
            + jnp.dot(lo, b_bf, preferred_element_type=f32))


def _dot_exact_lhs(a_bf, b):
    hi, mid, lo = _split3(b)
    return (jnp.dot(a_bf, hi, preferred_element_type=f32)
            + jnp.dot(a_bf, mid, preferred_element_type=f32)
            + jnp.dot(a_bf, lo, preferred_element_type=f32))


def _dot_hi(a, b):
    ah, al = _split2(a)
    bh, bl = _split2(b)
    return (jnp.dot(ah, bh, preferred_element_type=f32)
            + jnp.dot(ah, bl, preferred_element_type=f32)
            + jnp.dot(al, bh, preferred_element_type=f32))


def _inproj_kernel(x_ref, g_ref, w_ref, o_ref, h_ref):
    @pl.when(pl.program_id(1) == 0)
    def _():
        x = x_ref[...]
        ms = jnp.mean(x * x, axis=-1, keepdims=True)
        h_ref[...] = (x * lax.rsqrt(ms + EPS) * g_ref[...]).astype(bf16)

    o_ref[...] = jnp.dot(h_ref[...], w_ref[...], preferred_element_type=f32)


def _inproj(x, g, w_bf, tm, tn):
    T, D = x.shape
    N = w_bf.shape[1]
    return pl.pallas_call(
        _inproj_kernel,
        grid=(T // tm, N // tn),
        in_specs=[
            pl.BlockSpec((tm, D), lambda i, j: (i, 0)),
            pl.BlockSpec((1, D), lambda i, j: (0, 0)),
            pl.BlockSpec((D, tn), lambda i, j: (0, j)),
        ],
        out_specs=pl.BlockSpec((tm, tn), lambda i, j: (i, j)),
        out_shape=jax.ShapeDtypeStruct((T, N), f32),
        scratch_shapes=[pltpu.VMEM((tm, D), bf16)],
        compiler_params=pltpu.CompilerParams(
            dimension_semantics=("parallel", "arbitrary"), vmem_limit_bytes=VMEM_LIMIT),
        name=f"inproj_{T}x{N}",
    )(x, g, w_bf)


def _ret_kernel(q_ref, k_ref, v_ref, gt_ref, cs_ref, sn_ref, mask_ref, dec_ref, gain_ref, s0_ref,
                y_ref, sout_ref, s_scr, *, nblk):
    c = pl.program_id(2)

    @pl.when(c == 0)
    def _():
        s_scr[...] = s0_ref[0, 0]

    q = q_ref[...]
    k = k_ref[...]
    cs = cs_ref[...]
    sn = sn_ref[...]
    half = RET_DK // 2
    qr = q * cs + pltpu.roll(q, half, 1) * sn
    kr = (k * cs + pltpu.roll(k, half, 1) * sn) * (RET_DK ** -0.5)
    dec = dec_ref[0]
    qdec = dec[:, 0:1]
    kdec = dec[:, 1:2]
    gblk = dec[0:1, 2:3]
    v = v_ref[...]
    s = s_scr[...]
    sc = _dot_nt(qr, kr) * mask_ref[0]
    o = _dot(sc, v) + _dot(qr * qdec, s)
    s_new = gblk * s + _dot_tn(kr * kdec, v)
    s_scr[...] = s_new
    ms = jnp.mean(o * o, axis=-1, keepdims=True)
    on = o * lax.rsqrt(ms + EPS) * gain_ref[...]
    gt = gt_ref[...]
    y_ref[...] = gt * jax.nn.sigmoid(gt) * on

    @pl.when(c == nblk - 1)
    def _():
        sout_ref[0, 0] = s_new


def _ret_tables(L, cb, pos0, chunked):
    half = RET_DK // 2
    inv = ROPE_BASE ** (-jnp.arange(half, dtype=f32) / half)
    pos = (pos0 + jnp.arange(L, dtype=jnp.int32)).astype(f32)
    ang = pos[:, None] * inv[None, :]
    cos = jnp.cos(ang)
    sin = jnp.sin(ang)
    cs = jnp.concatenate([cos, cos], -1)
    sn = jnp.concatenate([-sin, sin], -1)
    log_gamma = jnp.log1p(-jnp.exp2(-5.0 - jnp.arange(RET_HEADS, dtype=f32)))
    idx = jnp.arange(cb, dtype=f32)
    dist = jnp.abs(idx[:, None] - idx[None, :])
    mask = jnp.exp(log_gamma[:, None, None] * dist[None])
    if chunked:
        ci = jnp.arange(cb) // CHUNK
        mask = jnp.where((ci[None, :] <= ci[:, None])[None], mask, 0.0)
    qdec = jnp.exp(log_gamma[:, None] * (idx[None, :] + 1.0))
    kdec = jnp.exp(log_gamma[:, None] * (cb - 1.0 - idx[None, :]))
    gblk = jnp.broadcast_to(jnp.exp(log_gamma * cb)[:, None], (RET_HEADS, cb))
    dec = jnp.zeros((RET_HEADS, cb, LANES), f32)
    dec = dec.at[:, :, 0].set(qdec).at[:, :, 1].set(kdec).at[:, :, 2].set(gblk)
    return cs, sn, mask, dec


def _retention(p_ret, row0, B, L, cb, pos0, s0, gain):
    nblk = L // cb
    rb0 = row0 // cb
    cs, sn, mask, dec = _ret_tables(L, cb, pos0, chunked=L > CHUNK)
    T = p_ret.shape[0]
    nq = RET_QK_W // RET_DK
    row = lambda b, h, c: rb0 + b * nblk + c
    y, s_out = pl.pallas_call(
        functools.partial(_ret_kernel, nblk=nblk),
        grid=(B, RET_HEADS, nblk),
        in_specs=[
            pl.BlockSpec((cb, RET_DK), lambda b, h, c: (row(b, h, c), h)),
            pl.BlockSpec((cb, RET_DK), lambda b, h, c: (row(b, h, c), nq + h)),
            pl.BlockSpec((cb, RET_DV), lambda b, h, c: (row(b, h, c), nq + h)),
            pl.BlockSpec((cb, RET_DV), lambda b, h, c: (row(b, h, c), 2 * nq + h)),
            pl.BlockSpec((cb, RET_DK), lambda b, h, c: (c, 0)),
            pl.BlockSpec((cb, RET_DK), lambda b, h, c: (c, 0)),
            pl.BlockSpec((1, cb, cb), lambda b, h, c: (h, 0, 0)),
            pl.BlockSpec((1, cb, LANES), lambda b, h, c: (h, 0, 0)),
            pl.BlockSpec((1, RET_DV), lambda b, h, c: (0, h)),
            pl.BlockSpec((1, 1, RET_DK, RET_DV), lambda b, h, c: (b, h, 0, 0)),
        ],
        out_specs=[
            pl.BlockSpec((cb, RET_DV), lambda b, h, c: (b * nblk + c, h)),
            pl.BlockSpec((1, 1, RET_DK, RET_DV), lambda b, h, c: (b, h, 0, 0)),
        ],
        out_shape=[
            jax.ShapeDtypeStruct((B * L, RET_V_W), f32),
            jax.ShapeDtypeStruct((B, RET_HEADS, RET_DK, RET_DV), f32),
        ],
        scratch_shapes=[pltpu.VMEM((RET_DK, RET_DV), f32)],
        compiler_params=pltpu.CompilerParams(
            dimension_semantics=("arbitrary", "arbitrary", "arbitrary"), vmem_limit_bytes=VMEM_LIMIT),
        name=f"retention_{L}",
    )(p_ret, p_ret, p_ret, p_ret, cs, sn, mask, dec, gain, s0)
    return y, s_out


def _rwkv_kernel(p_ref, sh0_ref, s0_ref, mu_ref, w0_ref, a0_ref, w2_ref, a2_ref, g2_ref, kk_ref, ka_ref,
                 rk_ref, lg_ref, lb_ref, bd_ref, y_ref, sout_ref, carry, s_scr, *, nvalid, nchunk):
    C = CHUNK
    c = pl.program_id(1)

    @pl.when(c == 0)
    def _():
        carry[...] = sh0_ref[0]
        s_scr[...] = s0_ref[0]

    if nvalid == C:
        pf = p_ref[...]
    else:
        pf = jnp.concatenate([p_ref[...], jnp.zeros((C - nvalid, RWKV_PROJ_PAD), f32)], axis=0)
    rows = lax.broadcasted_iota(jnp.int32, (C, 1), 0)
    prev = jnp.where(rows == 0, carry[...], pltpu.roll(pf, 1, 0))
    pm = pf + (prev - pf) * mu_ref[...]
    carry[...] = pf[nvalid - 1:nvalid, :]

    o1, o2, o3 = RWKV_W, 2 * RWKV_W, 3 * RWKV_W
    r = pm[:, :o1]
    k = pm[:, o1:o2]
    v = pm[:, o2:o3]
    xwa = pm[:, o3:o3 + W_LORA + A_LORA]
    xg = pm[:, o3 + W_LORA + A_LORA:o3 + W_LORA + A_LORA + 2 * LANES]
    z = w0_ref[...] + _dot_hi(jnp.tanh(xwa), w2_ref[...])
    softplus = jnp.maximum(-z, 0.0) + jnp.log1p(jnp.exp(-jnp.abs(z)))
    ew = jnp.exp(-softplus - 0.5)
    a = jax.nn.sigmoid(a0_ref[...] + _dot_hi(xwa, a2_ref[...]))
    g = _dot_hi(jax.nn.sigmoid(xg), g2_ref[...])
    k2 = k * (1.0 + (a - 1.0) * ka_ref[...])
    kkr = k * kk_ref[...]
    if nvalid != C:
        valid = rows < nvalid
        ew = jnp.where(valid, ew, 0.0)
        k2 = jnp.where(valid, k2, 0.0)
        v = jnp.where(valid, v, 0.0)
        a = jnp.where(valid, a, 0.0)

    ti = lax.broadcasted_iota(jnp.int32, (C, C), 0)
    tj = lax.broadcasted_iota(jnp.int32, (C, C), 1)
    tri = jnp.where(tj <= ti, 1.0, 0.0).astype(bf16)
    cum = _dot_exact_lhs(tri, -ew)
    pt = jnp.exp(cum)
    pprev = jnp.exp(cum + ew)
    pinv = jnp.exp(-cum)
    clast = cum[C - 1:C, :]
    pcinv = jnp.exp(clast - cum)
    pc = jnp.exp(clast)
    rt_all = r * pt
    kt_all = k2 * pinv
    kp_all = k2 * pcinv
    rkk = r * k2 * rk_ref[...]

    lane = lax.broadcasted_iota(jnp.int32, (C, 2 * RWKV_HEAD), 1)
    m0 = lane < RWKV_HEAD
    gi = lax.broadcasted_iota(jnp.int32, (4 * C, 4 * C), 0)
    gj = lax.broadcasted_iota(jnp.int32, (4 * C, 4 * C), 1)
    gil = gi % C
    gjl = gj % C
    gmask = gjl < gil + jnp.where(gi < 2 * C, 0, 1)
    ei = lax.broadcasted_iota(jnp.int32, (2 * C, 2 * C), 0)
    ej = lax.broadcasted_iota(jnp.int32, (2 * C, 2 * C), 1)
    eye = jnp.where(ei == ej, 1.0, 0.0)
    bd = bd_ref[...]

    def stack(x):
        return jnp.concatenate([jnp.where(m0, x, 0.0), jnp.where(m0, 0.0, x)], axis=0)

    def gsum(x):
        hi, lo = _split2(x)
        return jnp.dot(hi, bd, preferred_element_type=f32) + jnp.dot(lo, bd, preferred_element_type=f32)

    nsq = int(math.log2(C))
    PR = range(RWKV_PAIRS)
    lss = [slice(p * 2 * RWKV_HEAD, (p + 1) * 2 * RWKV_HEAD) for p in PR]
    kkp = [kkr[:, ls] for ls in lss]
    nrm = [jnp.sqrt(gsum(x * x)) for x in kkp]
    kk = [kkp[p] / jnp.maximum(nrm[p], 1e-12) for p in PR]
    bb = [kk[p] * a[:, lss[p]] for p in PR]
    ar = [jnp.concatenate([stack(-kk[p] * pprev[:, lss[p]]), stack(rt_all[:, lss[p]])], axis=0).astype(bf16)
          for p in PR]
    bk = [jnp.concatenate([stack(bb[p] * pinv[:, lss[p]]), stack(kt_all[:, lss[p]])], axis=0).astype(bf16)
          for p in PR]
    bkp = [jnp.concatenate([stack(bb[p] * pcinv[:, lss[p]]), stack(kp_all[:, lss[p]])], axis=0).astype(bf16)
           for p in PR]
    v_s = [stack(v[:, ls]) for ls in lss]
    s0 = [s_scr[p] for p in PR]
    gm = [jnp.where(gmask, _dot_nt(ar[p], bk[p]), 0.0) for p in PR]
    arh = [_dot_nt(ar[p], s0[p]) for p in PR]
    rhs = [arh[p][:2 * C] + _dot(gm[p][:2 * C, 2 * C:], v_s[p]) for p in PR]
    rbk = [gm[p][2 * C:, :].astype(bf16) for p in PR]
    npow = [gm[p][:2 * C, :2 * C] for p in PR]
    tinv = [eye + npow[p] for p in PR]
    for i in range(1, nsq):
        npow = [_dot(n, n) for n in npow]
        tinv = [tinv[p] + _dot(npow[p], tinv[p]) for p in PR]
    u_s = [_dot(tinv[p], rhs[p]) for p in PR]
    uv = [jnp.concatenate([u_s[p], v_s[p]], axis=0).astype(bf16) for p in PR]
    y_s = [arh[p][2 * C:] + _dot(rbk[p], uv[p]) for p in PR]
    for p in PR:
        s_scr[p] = s0[p] * pc[:, lss[p]] + _dot_tn(uv[p], bkp[p])
    yp = [y[:C] + y[C:] for y in y_s]
    mean = [gsum(y) * (1.0 / RWKV_HEAD) for y in yp]
    yc = [yp[p] - mean[p] for p in PR]
    var = [gsum(y * y) * (1.0 / RWKV_HEAD) for y in yc]
    bonus = [gsum(rkk[:, ls]) for ls in lss]
    for p in PR:
        ls = lss[p]
        yn = yc[p] * lax.rsqrt(var[p] + RWKV_GN_EPS) * lg_ref[:, ls] + lb_ref[:, ls]
        yo = (yn + bonus[p] * v[:, ls]) * g[:, ls]
        y_ref[:, ls] = yo[:nvalid]

    @pl.when(c == nchunk - 1)
    def _():
        sout_ref[0] = s_scr[...]


def _pair_states(s):
    B = s.shape[0]
    sp = s.reshape(B, RWKV_PAIRS, 2, RWKV_HEAD, RWKV_HEAD)
    z = jnp.zeros_like(sp[:, :, 0])
    top = jnp.concatenate([sp[:, :, 0], z], -1)
    bot = jnp.concatenate([z, sp[:, :, 1]], -1)
    return jnp.concatenate([top, bot], -2)


def _unpair_states(sp):
    B = sp.shape[0]
    h = RWKV_HEAD
    return jnp.stack([sp[:, :, :h, :h], sp[:, :, h:, h:]], 2).reshape(B, RWKV_HEADS, h, h)


def _rwkv(p_rwkv, row0, B, L, sh0, s0, consts):
    nvalid = min(L, CHUNK)
    nchunk = L // nvalid
    rb0 = row0 // nvalid
    full = lambda shape: pl.BlockSpec(shape, lambda b, c: (0,) * len(shape))
    in_specs = [
        pl.BlockSpec((nvalid, RWKV_PROJ_PAD), lambda b, c: (rb0 + b * nchunk + c, 0)),
        pl.BlockSpec((1, 1, RWKV_PROJ_PAD), lambda b, c: (b, 0, 0)),
        pl.BlockSpec((1, RWKV_PAIRS, 2 * RWKV_HEAD, 2 * RWKV_HEAD), lambda b, c: (b, 0, 0, 0)),
    ] + [full(x.shape) for x in consts]
    y, s_out = pl.pallas_call(
        functools.partial(_rwkv_kernel, nvalid=nvalid, nchunk=nchunk),
        grid=(B, nchunk),
        in_specs=in_specs,
        out_specs=[
            pl.BlockSpec((nvalid, RWKV_W), lambda b, c: (b * nchunk + c, 0)),
            pl.BlockSpec((1, RWKV_PAIRS, 2 * RWKV_HEAD, 2 * RWKV_HEAD), lambda b, c: (b, 0, 0, 0)),
        ],
        out_shape=[
            jax.ShapeDtypeStruct((B * L, RWKV_W), f32),
            jax.ShapeDtypeStruct((B, RWKV_PAIRS, 2 * RWKV_HEAD, 2 * RWKV_HEAD), f32),
        ],
        scratch_shapes=[
            pltpu.VMEM((1, RWKV_PROJ_PAD), f32),
            pltpu.VMEM((RWKV_PAIRS, 2 * RWKV_HEAD, 2 * RWKV_HEAD), f32),
        ],
        compiler_params=pltpu.CompilerParams(
            dimension_semantics=("arbitrary", "arbitrary"), vmem_limit_bytes=VMEM_LIMIT),
        name=f"rwkv7_{L}",
    )(p_rwkv, sh0, _pair_states(s0), *consts)
    return y, _unpair_states(s_out)


def _outproj_kernel(yr_ref, yw_ref, x_ref, wr_ref, ww_ref, g_ref, x1_ref, h_ref):
    x1 = (x_ref[...] + jnp.dot(yr_ref[...].astype(bf16), wr_ref[...], preferred_element_type=f32)
          + jnp.dot(yw_ref[...].astype(bf16), ww_ref[...], preferred_element_type=f32))
    x1_ref[...] = x1
    ms = jnp.mean(x1 * x1, axis=-1, keepdims=True)
    h_ref[...] = x1 * lax.rsqrt(ms + EPS) * g_ref[...]


def _outproj(y_ret, y_rwkv, x, w_ret_bf, w_rwkv_bf, g, tm):
    T, D = x.shape
    row = lambda shape: pl.BlockSpec(shape, lambda i: (i, 0))
    full = lambda shape: pl.BlockSpec(shape, lambda i: (0, 0))
    return pl.pallas_call(
        _outproj_kernel,
        grid=(T // tm,),
        in_specs=[row((tm, RET_V_W)), row((tm, RWKV_W)), row((tm, D)),
                  full(w_ret_bf.shape), full(w_rwkv_bf.shape), full((1, D))],
        out_specs=[row((tm, D)), row((tm, D))],
        out_shape=[jax.ShapeDtypeStruct((T, D), f32), jax.ShapeDtypeStruct((T, D), f32)],
        compiler_params=pltpu.CompilerParams(
            dimension_semantics=("parallel",), vmem_limit_bytes=VMEM_LIMIT),
        name=f"outproj_{T}",
    )(y_ret, y_rwkv, x, w_ret_bf, w_rwkv_bf, g)


def _topk_multi(s_list, pos_list, vals_refs, idx_refs, lanes_list):
    n = len(s_list)
    big = 3.0e38
    for it in range(PEER_TOPK):
        m = [jnp.max(s, axis=0, keepdims=True) for s in s_list]
        idx = [jnp.min(jnp.where(s_list[q] == m[q], pos_list[q], big), axis=0, keepdims=True) for q in range(n)]
        for q in range(n):
            vals_refs[q][it:it + 1, lanes_list[q]] = m[q]
            idx_refs[q][it:it + 1, lanes_list[q]] = idx[q]
        s_list = [jnp.where(pos_list[q] == idx[q], -jnp.inf, s_list[q]) for q in range(n)]


def _cand_positions():
    K = PEER_TOPK
    pos = [float(j) for j in range(K)]
    for i in range(1, 8):
        pos += [float(i * K + j) if j < K // (i + 1) else -1.0 for j in range(8)]
    pos += [float(i * K) for i in range(8, K)]
    return np.asarray(pos, np.float32)


N_CAND = 16 + 7 * 8 + 8


def _peer_topk_kernel(h_ref, wq_ref, sk_ref, cpos_ref, e_ref, g_ref, v1, i1, v2, i2, tv, tp, *, tm):
    q = jnp.dot(h_ref[...].astype(bf16), wq_ref[...], preferred_element_type=f32)
    K = PEER_TOPK
    nt = tm // LANES
    tiles = [slice(lt * LANES, (lt + 1) * LANES) for lt in range(nt)]
    key_iota = lax.broadcasted_iota(jnp.int32, (N_KEYS, LANES), 0).astype(f32)
    s_list, vrefs, irefs, lanes_list = [], [], [], []
    for lanes in tiles:
        for half, (vr, ir) in enumerate(((v1, i1), (v2, i2))):
            qh = q[lanes, half * PEER_HALF:(half + 1) * PEER_HALF]
            s_list.append(_dot_nt(sk_ref[0, half], qh))
            vrefs.append(vr)
            irefs.append(ir)
            lanes_list.append(lanes)
    _topk_multi(s_list, [key_iota] * len(s_list), vrefs, irefs, lanes_list)

    cpos = cpos_ref[...]
    cands = []
    for lanes in tiles:
        a = v1[:, lanes]
        b = v2[:, lanes]
        blocks = [a[0:1, :] + b]
        blocks += [a[i:i + 1, :] + b[0:8, :] for i in range(1, 8)]
        blocks.append(a[8:K, :] + b[0:1, :])
        cands.append(jnp.where(cpos >= 0.0, jnp.concatenate(blocks, axis=0), -jnp.inf))
    _topk_multi(cands, [cpos] * nt, [tv] * nt, [tp] * nt, tiles)

    for lanes in tiles:
        pos = tp[:, lanes]
        pi = jnp.floor(pos * (1.0 / K))
        pj = pos - pi * K
        e1 = jnp.zeros((K, LANES), f32)
        e2 = jnp.zeros((K, LANES), f32)
        for i in range(K):
            e1 = e1 + jnp.where(pi == float(i), i1[i:i + 1, lanes], 0.0)
            e2 = e2 + jnp.where(pj == float(i), i2[i:i + 1, lanes], 0.0)
        e_ref[:, lanes] = (e1 * float(N_KEYS) + e2).astype(jnp.int32)
        top = tv[:, lanes]
        ex = jnp.exp(top - top[0:1, :])
        g_ref[:, lanes] = ex / jnp.sum(ex, axis=0, keepdims=True)


def _peer_topk(h2, wq_bf, sk_bf, tm):
    T, D = h2.shape
    K = PEER_TOPK
    cpos = jnp.asarray(np.broadcast_to(_cand_positions()[:, None], (N_CAND, LANES)))
    return pl.pallas_call(
        functools.partial(_peer_topk_kernel, tm=tm),
        grid=(T // tm, PEER_HEADS),
        in_specs=[
            pl.BlockSpec((tm, D), lambda i, h: (i, 0)),
            pl.BlockSpec((D, PEER_DK), lambda i, h: (0, h)),
            pl.BlockSpec((1, 2, N_KEYS, PEER_HALF), lambda i, h: (h, 0, 0, 0)),
            pl.BlockSpec((N_CAND, LANES), lambda i, h: (0, 0)),
        ],
        out_specs=[
            pl.BlockSpec((K, tm), lambda i, h: (h, i)),
            pl.BlockSpec((K, tm), lambda i, h: (h, i)),
        ],
        out_shape=[jax.ShapeDtypeStruct((PEER_SLOTS, T), jnp.int32),
                   jax.ShapeDtypeStruct((PEER_SLOTS, T), f32)],
        scratch_shapes=[pltpu.VMEM((K, tm), f32)] * 6,
        compiler_params=pltpu.CompilerParams(
            dimension_semantics=("parallel", "arbitrary"), vmem_limit_bytes=VMEM_LIMIT),
        name=f"peer_topk_{T}",
    )(h2, wq_bf, sk_bf, cpos)


PEER_NBUF = 8
PEER_GROUP = 8
DS = D_MODEL // LANES
EROWS = 2 * DS


def _peer_mix_kernel(idx_ref, gate_ref, h_ref, x1_ref, fg_ref, uv_hbm, o_ref, *scratch, tb):
    bufs = scratch[:PEER_NBUF]
    z_scr, sem = scratch[PEER_NBUF:]
    S = PEER_SLOTS
    tile = lambda s: slice(s * LANES, (s + 1) * LANES)
    G = PEER_GROUP
    ngroups = tb // G

    def issue(t, slot):
        for e in range(S):
            pltpu.make_async_copy(uv_hbm.at[idx_ref[t, e]], bufs[slot].at[:, e, :], sem.at[slot]).start()

    def wait(slot):
        pltpu.make_async_copy(bufs[slot], bufs[slot], sem.at[slot]).wait()

    for t in range(PEER_NBUF):
        issue(t, t)

    fg = fg_ref[...]

    def group(t0, first, last):
        gcols = jnp.transpose(gate_ref[pl.ds(t0, G), :])
        for j in range(G):
            t = t0 + j
            wait(j)
            if j == 0:
                if not first:
                    issue(t + G - 1, G - 1)
            elif not last:
                issue(t + G - 1, j - 1)
            buf = bufs[j]
            xt = h_ref[pl.ds(t, 1), :]
            x1t = x1_ref[pl.ds(t, 1), :]
            acc = buf[0] * xt[:, tile(0)]
            for s in range(1, DS):
                acc = acc + buf[s] * xt[:, tile(s)]
            a = jnp.sum(acc, axis=1, keepdims=True)
            act = 0.5 * a * (1.0 + lax.erf(a * (2.0 ** -0.5)))
            cw = jnp.broadcast_to(gcols[:, j:j + 1] * act, (S, LANES))
            for s in range(DS):
                z_scr[j:j + 1, tile(s)] = x1t[:, tile(s)] + jnp.sum(buf[DS + s] * cw, axis=0, keepdims=True)
        z = z_scr[...]
        ms = jnp.mean(z * z, axis=-1, keepdims=True)
        o_ref[pl.ds(t0, G), :] = z * lax.rsqrt(ms + EPS) * fg

    group(0, True, False)

    def mid(gi, carry):
        group(pl.multiple_of(gi * G, G), False, False)
        return carry

    lax.fori_loop(1, ngroups - 1, mid, 0)
    group((ngroups - 1) * G, False, True)


def _expert_slabs(expert_u, expert_v):
    E = expert_u.shape[0]
    return jnp.concatenate([expert_u.reshape(E, DS, LANES), expert_v.reshape(E, DS, LANES)], 1)


def _peer_mix(idx, gates, h2, x1, fg, uv, tb):
    T, D = h2.shape
    S = PEER_SLOTS
    assert PEER_GROUP == PEER_NBUF and tb % PEER_GROUP == 0 and tb >= 2 * PEER_GROUP
    rows = pl.BlockSpec((tb, D), lambda i: (i, 0))
    return pl.pallas_call(
        functools.partial(_peer_mix_kernel, tb=tb),
        grid=(T // tb,),
        in_specs=[
            pl.BlockSpec((tb, S), lambda i: (i, 0), memory_space=pltpu.SMEM),
            pl.BlockSpec((tb, S), lambda i: (i, 0)),
            rows,
            rows,
            pl.BlockSpec((1, D), lambda i: (0, 0)),
            pl.BlockSpec(memory_space=pl.ANY),
        ],
        out_specs=rows,
        out_shape=jax.ShapeDtypeStruct((T, D), f32),
        scratch_shapes=[pltpu.VMEM((EROWS, S, LANES), f32)] * PEER_NBUF + [
            pltpu.VMEM((PEER_GROUP, D), f32),
            pltpu.SemaphoreType.DMA((PEER_NBUF,)),
        ],
        compiler_params=pltpu.CompilerParams(
            dimension_semantics=("arbitrary",), vmem_limit_bytes=VMEM_LIMIT),
        name=f"peer_mix_{T}",
    )(idx, gates, h2, x1, fg.reshape(1, D), uv)


def _rwkv_consts(mu, w0, w2, a0, a2, g2, k_k, k_a, r_k, lnx_g, lnx_b):
    pad = RWKV_PROJ_PAD - RWKV_PROJ
    mu_p = jnp.pad(mu, (0, pad)).reshape(1, RWKV_PROJ_PAD)
    w2_p = jnp.concatenate([w2, jnp.zeros((A_LORA, RWKV_W), f32)], 0)
    a2_p = jnp.concatenate([jnp.zeros((W_LORA, RWKV_W), f32), a2], 0)
    g2_p = jnp.concatenate([g2, jnp.zeros((2 * LANES - G_LORA, RWKV_W), f32)], 0)
    hd = np.arange(2 * RWKV_HEAD) // RWKV_HEAD
    bd = jnp.asarray((hd[:, None] == hd[None, :]).astype(np.float32), dtype=bf16)
    r1 = lambda x: x.reshape(1, RWKV_W)
    return (mu_p, r1(w0), r1(a0), w2_p, a2_p, g2_p, r1(k_k), r1(k_a), r1(r_k), r1(lnx_g), r1(lnx_b), bd)


def kernel(x_prompt, x_sample, state_ret, state_rwkv, state_rwkv_shift, norm1_g, w_in, ret_gain, rwkv_mu,
           rwkv_w0, rwkv_w2, rwkv_a0, rwkv_a2, rwkv_g2, rwkv_k_k, rwkv_k_a, rwkv_r_k, lnx_g, lnx_b, w_out,
           norm2_g, peer_w_query, peer_sub_keys, peer_u, peer_v, final_g):
    Bp, Lp, D = x_prompt.shape
    Bs, Ls, _ = x_sample.shape
    past_len = 2048
    pad = RWKV_PROJ_PAD - RWKV_PROJ

    l = 0
    w_in_l = w_in[l]
    w_ret = w_in_l[:, :RET_PROJ].astype(bf16)
    w_rwkv = jnp.pad(w_in_l[:, RET_PROJ:], ((0, 0), (0, pad))).astype(bf16)
    g1 = norm1_g[l].reshape(1, D)
    gain = ret_gain[l].reshape(1, RET_V_W)
    consts = _rwkv_consts(rwkv_mu[l], rwkv_w0[l], rwkv_w2[l], rwkv_a0[l], rwkv_a2[l], rwkv_g2[l], rwkv_k_k[l],
                          rwkv_k_a[l], rwkv_r_k[l].reshape(-1), lnx_g[l], lnx_b[l])
    w_out_l = w_out[l].astype(bf16)
    g2 = norm2_g[l].reshape(1, D)
    wq = peer_w_query[l].astype(bf16)
    sk = peer_sub_keys[l].astype(bf16)
    uv = _expert_slabs(peer_u[l], peer_v[l])

    def layer(xg, pos0, s_ret, s_rwkv, shift, ret_block, tm):
        B, L, _ = xg.shape
        x = xg.reshape(B * L, D)
        p_ret = _inproj(x, g1, w_ret, tm, 512)
        p_rwkv = _inproj(x, g1, w_rwkv, tm, 512)
        y_ret, s_ret_new = _retention(p_ret, 0, B, L, ret_block, pos0, s_ret, gain)
        y_rwkv, s_rwkv_new = _rwkv(p_rwkv, 0, B, L, jnp.pad(shift, ((0, 0), (0, 0), (0, pad))), s_rwkv, consts)
        x1, h2 = _outproj(y_ret, y_rwkv, x, w_out_l[:RET_V_W], w_out_l[RET_V_W:], g2, 512)
        e_t, g_t = _peer_topk(h2, wq, sk, 256)
        y = _peer_mix(e_t.T, g_t.T, h2, x1, final_g, uv, 256)
        shift_new = p_rwkv.reshape(B, L, RWKV_PROJ_PAD)[:, -1:, :RWKV_PROJ]
        return y.reshape(B, L, D), s_ret_new[None], s_rwkv_new[None], shift_new[None]

    y_p, sr_p, sw_p, sh_p = layer(x_prompt, 0, jnp.zeros((Bp, RET_HEADS, RET_DK, RET_DV), f32),
                                  jnp.zeros((Bp, RWKV_HEADS, RWKV_HEAD, RWKV_HEAD), f32),
                                  jnp.zeros((Bp, 1, RWKV_PROJ), f32), 256, 1024)
    y_s, sr_s, sw_s, sh_s = layer(x_sample, past_len, state_ret[l], state_rwkv[l], state_rwkv_shift[l], Ls, 1024)
    return (y_p, y_s, sr_p, sw_p, sh_p, sr_s, sw_s, sh_s)
```

```python
import functools
import math

import numpy as np
import jax
import jax.numpy as jnp
from jax import lax
from jax.experimental import pallas as pl
from jax.experimental.pallas import tpu as pltpu

f32 = jnp.float32
bf16 = jnp.bfloat16

D_MODEL = 2048
CHUNK = 64
EPS = 1e-6
RET_HEADS = 4
RET_DK = 128
RET_DV = 256
RET_QK_W = RET_HEADS * RET_DK
RET_V_W = RET_HEADS * RET_DV
RET_PROJ = 2 * RET_QK_W + 2 * RET_V_W
ROPE_BASE = 10000.0
RWKV_HEAD = 64
RWKV_W = D_MODEL // 2
RWKV_HEADS = RWKV_W // RWKV_HEAD
RWKV_PAIRS = RWKV_HEADS // 2
W_LORA = 64
A_LORA = 64
G_LORA = 160
RWKV_PROJ = 3 * RWKV_W + W_LORA + A_LORA + G_LORA
RWKV_PROJ_PAD = 3584
RWKV_GN_EPS = 64e-5
PEER_HEADS = 8
N_KEYS = 128
PEER_DK = 256
PEER_HALF = PEER_DK // 2
PEER_TOPK = 16
PEER_SLOTS = PEER_HEADS * PEER_TOPK

LANES = 128
VMEM_LIMIT = 56 * 1024 * 1024


def _dot(a, b):
    return jnp.dot(a.astype(bf16), b.astype(bf16), preferred_element_type=f32)


def _dot_nt(a, b):
    return lax.dot_general(a.astype(bf16), b.astype(bf16), (((1,), (1,)), ((), ())),
                           preferred_element_type=f32)


def _dot_tn(a, b):
    return lax.dot_general(a.astype(bf16), b.astype(bf16), (((0,), (0,)), ((), ())),
                           preferred_element_type=f32)


def _split2(x):
    hi = x.astype(bf16)
    lo = (x - hi.astype(f32)).astype(bf16)
    return hi, lo


def _split3(x):
    hi = x.astype(bf16)
    r1 = x - hi.astype(f32)
    mid = r1.astype(bf16)
    lo = (r1 - mid.astype(f32)).astype(bf16)
    return hi, mid, lo


def _dot_exact_rhs(a, b_bf):
    hi, mid, lo = _split3(a)
    return (jnp.dot(hi, b_bf, preferred_element_type=f32)
            + jnp.dot(mid, b_bf, preferred_element_type=f32)
            + jnp.dot(lo, b_bf, preferred_element_type=f32))


def _dot_exact_lhs(a_bf, b):
    hi, mid, lo = _split3(b)
    return (jnp.dot(a_bf, hi, preferred_element_type=f32)
            + jnp.dot(a_bf, mid, preferred_element_type=f32)
            + jnp.dot(a_bf, lo, preferred_element_type=f32))


def _dot_hi(a, b):
    ah, al = _split2(a)
    bh, bl = _split2(b)
    return (jnp.dot(ah, bh, preferred_element_type=f32)
            + jnp.dot(ah, bl, preferred_element_type=f32)
            + jnp.dot(al, bh, preferred_element_type=f32))


def _inproj_kernel(x_ref, g_ref, w_ref, o_ref, h_ref):
    @pl.when(pl.program_id(1) == 0)
    def _():
        x = x_ref[...]
        ms = jnp.mean(x * x, axis=-1, keepdims=True)
        h_ref[...] = (x * lax.rsqrt(ms + EPS) * g_ref[...]).astype(bf16)

    o_ref[...] = jnp.dot(h_ref[...], w_ref[...], preferred_element_type=f32)


def _inproj(x, g, w_bf, tm, tn):
    T, D = x.shape
    N = w_bf.shape[1]
    return pl.pallas_call(
        _inproj_kernel,
        grid=(T // tm, N // tn),
        in_specs=[
            pl.BlockSpec((tm, D), lambda i, j: (i, 0)),
            pl.BlockSpec((1, D), lambda i, j: (0, 0)),
            pl.BlockSpec((D, tn), lambda i, j: (0, j)),
        ],
        out_specs=pl.BlockSpec((tm, tn), lambda i, j: (i, j)),
        out_shape=jax.ShapeDtypeStruct((T, N), f32),
        scratch_shapes=[pltpu.VMEM((tm, D), bf16)],
        compiler_params=pltpu.CompilerParams(
            dimension_semantics=("parallel", "arbitrary"), vmem_limit_bytes=VMEM_LIMIT),
        name=f"inproj_{T}x{N}",
    )(x, g, w_bf)


def _ret_kernel(q_ref, k_ref, v_ref, gt_ref, cs_ref, sn_ref, mask_ref, dec_ref, gain_ref, s0_ref,
                y_ref, sout_ref, s_scr, *, nblk):
    c = pl.program_id(1)

    @pl.when(c == 0)
    def _():
        s_scr[...] = s0_ref[0]

    HR = range(RET_HEADS)
    cs = cs_ref[...]
    sn = sn_ref[...]
    half = RET_DK // 2
    ksl = [slice(h * RET_DK, (h + 1) * RET_DK) for h in HR]
    vsl = [slice(h * RET_DV, (h + 1) * RET_DV) for h in HR]
    q = [q_ref[:, ksl[h]] for h in HR]
    k = [k_ref[:, ksl[h]] for h in HR]
    qr = [q[h] * cs + pltpu.roll(q[h], half, 1) * sn for h in HR]
    kr = [(k[h] * cs + pltpu.roll(k[h], half, 1) * sn) * (RET_DK ** -0.5) for h in HR]
    dec = [dec_ref[h] for h in HR]
    v = [v_ref[:, vsl[h]].astype(bf16) for h in HR]
    s = [s_scr[h] for h in HR]
    sc = [_dot_nt(qr[h], kr[h]) * mask_ref[h] for h in HR]
    o = [_dot(sc[h], v[h]) + _dot(qr[h] * dec[h][:, 0:1], s[h]) for h in HR]
    s_new = [dec[h][0:1, 2:3] * s[h] + _dot_tn(kr[h] * dec[h][:, 1:2], v[h]) for h in HR]
    for h in HR:
        s_scr[h] = s_new[h]
        ms = jnp.mean(o[h] * o[h], axis=-1, keepdims=True)
        on = o[h] * lax.rsqrt(ms + EPS) * gain_ref[:, vsl[h]]
        gt = gt_ref[:, vsl[h]]
        y_ref[:, vsl[h]] = gt * jax.nn.sigmoid(gt) * on

    @pl.when(c == nblk - 1)
    def _():
        sout_ref[0] = s_scr[...]


def _ret_tables(L, cb, pos0, chunked):
    half = RET_DK // 2
    inv = ROPE_BASE ** (-jnp.arange(half, dtype=f32) / half)
    pos = (pos0 + jnp.arange(L, dtype=jnp.int32)).astype(f32)
    ang = pos[:, None] * inv[None, :]
    cos = jnp.cos(ang)
    sin = jnp.sin(ang)
    cs = jnp.concatenate([cos, cos], -1)
    sn = jnp.concatenate([-sin, sin], -1)
    log_gamma = jnp.log1p(-jnp.exp2(-5.0 - jnp.arange(RET_HEADS, dtype=f32)))
    idx = jnp.arange(cb, dtype=f32)
    dist = jnp.abs(idx[:, None] - idx[None, :])
    mask = jnp.exp(log_gamma[:, None, None] * dist[None])
    if chunked:
        ci = jnp.arange(cb) // CHUNK
        mask = jnp.where((ci[None, :] <= ci[:, None])[None], mask, 0.0)
    qdec = jnp.exp(log_gamma[:, None] * (idx[None, :] + 1.0))
    kdec = jnp.exp(log_gamma[:, None] * (cb - 1.0 - idx[None, :]))
    gblk = jnp.broadcast_to(jnp.exp(log_gamma * cb)[:, None], (RET_HEADS, cb))
    dec = jnp.zeros((RET_HEADS, cb, LANES), f32)
    dec = dec.at[:, :, 0].set(qdec).at[:, :, 1].set(kdec).at[:, :, 2].set(gblk)
    return cs, sn, mask, dec


def _retention(p_ret, row0, B, L, cb, pos0, s0, gain):
    nblk = L // cb
    rb0 = row0 // cb
    cs, sn, mask, dec = _ret_tables(L, cb, pos0, chunked=L > CHUNK)
    row = lambda b, c: rb0 + b * nblk + c
    y, s_out = pl.pallas_call(
        functools.partial(_ret_kernel, nblk=nblk),
        grid=(B, nblk),
        in_specs=[
            pl.BlockSpec((cb, RET_QK_W), lambda b, c: (row(b, c), 0)),
            pl.BlockSpec((cb, RET_QK_W), lambda b, c: (row(b, c), 1)),
            pl.BlockSpec((cb, RET_V_W), lambda b, c: (row(b, c), 1)),
            pl.BlockSpec((cb, RET_V_W), lambda b, c: (row(b, c), 2)),
            pl.BlockSpec((cb, RET_DK), lambda b, c: (c, 0)),
            pl.BlockSpec((cb, RET_DK), lambda b, c: (c, 0)),
            pl.BlockSpec((RET_HEADS, cb, cb), lambda b, c: (0, 0, 0)),
            pl.BlockSpec((RET_HEADS, cb, LANES), lambda b, c: (0, 0, 0)),
            pl.BlockSpec((1, RET_V_W), lambda b, c: (0, 0)),
            pl.BlockSpec((1, RET_HEADS, RET_DK, RET_DV), lambda b, c: (b, 0, 0, 0)),
        ],
        out_specs=[
            pl.BlockSpec((cb, RET_V_W), lambda b, c: (b * nblk + c, 0)),
            pl.BlockSpec((1, RET_HEADS, RET_DK, RET_DV), lambda b, c: (b, 0, 0, 0)),
        ],
        out_shape=[
            jax.ShapeDtypeStruct((B * L, RET_V_W), f32),
            jax.ShapeDtypeStruct((B, RET_HEADS, RET_DK, RET_DV), f32),
        ],
        scratch_shapes=[pltpu.VMEM((RET_HEADS, RET_DK, RET_DV), f32)],
        compiler_params=pltpu.CompilerParams(
            dimension_semantics=("arbitrary", "arbitrary"), vmem_limit_bytes=VMEM_LIMIT),
        name=f"retention_{L}",
    )(p_ret, p_ret, p_ret, p_ret, cs, sn, mask, dec, gain, s0)
    return y, s_out


def _rwkv_kernel(p_ref, sh0_ref, s0_ref, mu_ref, w0_ref, a0_ref, w2_ref, a2_ref, g2_ref, kk_ref, ka_ref,
                 rk_ref, lg_ref, lb_ref, bd_ref, y_ref, sout_ref, carry, s_scr, *, nvalid, nchunk):
    C = CHUNK
    c = pl.program_id(1)

    @pl.when(c == 0)
    def _():
        carry[...] = sh0_ref[0]
        s_scr[...] = s0_ref[0]

    if nvalid == C:
        pf = p_ref[...]
    else:
        pf = jnp.concatenate([p_ref[...], jnp.zeros((C - nvalid, RWKV_PROJ_PAD), f32)], axis=0)
    rows = lax.broadcasted_iota(jnp.int32, (C, 1), 0)
    prev = jnp.where(rows == 0, carry[...], pltpu.roll(pf, 1, 0))
    pm = pf + (prev - pf) * mu_ref[...]
    carry[...] = pf[nvalid - 1:nvalid, :]

    o1, o2, o3 = RWKV_W, 2 * RWKV_W, 3 * RWKV_W
    r = pm[:, :o1]
    k = pm[:, o1:o2]
    v = pm[:, o2:o3]
    xwa = pm[:, o3:o3 + W_LORA + A_LORA]
    xg = pm[:, o3 + W_LORA + A_LORA:o3 + W_LORA + A_LORA + 2 * LANES]
    z = w0_ref[...] + _dot_hi(jnp.tanh(xwa), w2_ref[...])
    softplus = jnp.maximum(-z, 0.0) + jnp.log1p(jnp.exp(-jnp.abs(z)))
    ew = jnp.exp(-softplus - 0.5)
    a = jax.nn.sigmoid(a0_ref[...] + _dot_hi(xwa, a2_ref[...]))
    g = _dot_hi(jax.nn.sigmoid(xg), g2_ref[...])
    k2 = k * (1.0 + (a - 1.0) * ka_ref[...])
    kkr = k * kk_ref[...]
    if nvalid != C:
        valid = rows < nvalid
        ew = jnp.where(valid, ew, 0.0)
        k2 = jnp.where(valid, k2, 0.0)
        v = jnp.where(valid, v, 0.0)
        a = jnp.where(valid, a, 0.0)

    ti = lax.broadcasted_iota(jnp.int32, (C, C), 0)
    tj = lax.broadcasted_iota(jnp.int32, (C, C), 1)
    tri = jnp.where(tj <= ti, 1.0, 0.0).astype(bf16)
    cum = _dot_exact_lhs(tri, -ew)
    pt = jnp.exp(cum)
    pprev = jnp.exp(cum + ew)
    pinv = jnp.exp(-cum)
    clast = cum[C - 1:C, :]
    pcinv = jnp.exp(clast - cum)
    pc = jnp.exp(clast)
    rt_all = r * pt
    kt_all = k2 * pinv
    kp_all = k2 * pcinv
    rkk = r * k2 * rk_ref[...]

    lane = lax.broadcasted_iota(jnp.int32, (C, 2 * RWKV_HEAD), 1)
    m0 = lane < RWKV_HEAD
    gi = lax.broadcasted_iota(jnp.int32, (4 * C, 4 * C), 0)
    gj = lax.broadcasted_iota(jnp.int32, (4 * C, 4 * C), 1)
    gil = gi % C
    gjl = gj % C
    gmask = gjl < gil + jnp.where(gi < 2 * C, 0, 1)
    ei = lax.broadcasted_iota(jnp.int32, (2 * C, 2 * C), 0)
    ej = lax.broadcasted_iota(jnp.int32, (2 * C, 2 * C), 1)
    eye = jnp.where(ei == ej, 1.0, 0.0)
    bd = bd_ref[...]

    def stack(x):
        return jnp.concatenate([jnp.where(m0, x, 0.0), jnp.where(m0, 0.0, x)], axis=0)

    def gsum(x):
        hi, lo = _split2(x)
        return jnp.dot(hi, bd, preferred_element_type=f32) + jnp.dot(lo, bd, preferred_element_type=f32)

    nsq = int(math.log2(C))
    PR = range(RWKV_PAIRS)
    lss = [slice(p * 2 * RWKV_HEAD, (p + 1) * 2 * RWKV_HEAD) for p in PR]
    kkp = [kkr[:, ls] for ls in lss]
    nrm = [jnp.sqrt(gsum(x * x)) for x in kkp]
    kk = [kkp[p] / jnp.maximum(nrm[p], 1e-12) for p in PR]
    bb = [kk[p] * a[:, lss[p]] for p in PR]
    ar = [jnp.concatenate([stack(-kk[p] * pprev[:, lss[p]]), stack(rt_all[:, lss[p]])], axis=0).astype(bf16)
          for p in PR]
    bk = [jnp.concatenate([stack(bb[p] * pinv[:, lss[p]]), stack(kt_all[:, lss[p]])], axis=0).astype(bf16)
          for p in PR]
    bkp = [jnp.concatenate([stack(bb[p] * pcinv[:, lss[p]]), stack(kp_all[:, lss[p]])], axis=0).astype(bf16)
           for p in PR]
    v_s = [stack(v[:, ls]) for ls in lss]
    s0 = [s_scr[p] for p in PR]
    gm = [jnp.where(gmask, _dot_nt(ar[p], bk[p]), 0.0) for p in PR]
    arh = [_dot_nt(ar[p], s0[p]) for p in PR]
    rhs = [arh[p][:2 * C] + _dot(gm[p][:2 * C, 2 * C:], v_s[p]) for p in PR]
    rbk = [gm[p][2 * C:, :].astype(bf16) for p in PR]
    npow = [gm[p][:2 * C, :2 * C] for p in PR]
    tinv = [eye + npow[p] for p in PR]
    for i in range(1, nsq):
        npow = [_dot(n, n) for n in npow]
        tinv = [tinv[p] + _dot(npow[p], tinv[p]) for p in PR]
    u_s = [_dot(tinv[p], rhs[p]) for p in PR]
    uv = [jnp.concatenate([u_s[p], v_s[p]], axis=0).astype(bf16) for p in PR]
    y_s = [arh[p][2 * C:] + _dot(rbk[p], uv[p]) for p in PR]
    for p in PR:
        s_scr[p] = s0[p] * pc[:, lss[p]] + _dot_tn(uv[p], bkp[p])
    yp = [y[:C] + y[C:] for y in y_s]
    mean = [gsum(y) * (1.0 / RWKV_HEAD) for y in yp]
    yc = [yp[p] - mean[p] for p in PR]
    var = [gsum(y * y) * (1.0 / RWKV_HEAD) for y in yc]
    bonus = [gsum(rkk[:, ls]) for ls in lss]
    for p in PR:
        ls = lss[p]
        yn = yc[p] * lax.rsqrt(var[p] + RWKV_GN_EPS) * lg_ref[:, ls] + lb_ref[:, ls]
        yo = (yn + bonus[p] * v[:, ls]) * g[:, ls]
        y_ref[:, ls] = yo[:nvalid]

    @pl.when(c == nchunk - 1)
    def _():
        sout_ref[0] = s_scr[...]


def _pair_states(s):
    B = s.shape[0]
    sp = s.reshape(B, RWKV_PAIRS, 2, RWKV_HEAD, RWKV_HEAD)
    z = jnp.zeros_like(sp[:, :, 0])
    top = jnp.concatenate([sp[:, :, 0], z], -1)
    bot = jnp.concatenate([z, sp[:, :, 1]], -1)
    return jnp.concatenate([top, bot], -2)


def _unpair_states(sp):
    B = sp.shape[0]
    h = RWKV_HEAD
    return jnp.stack([sp[:, :, :h, :h], sp[:, :, h:, h:]], 2).reshape(B, RWKV_HEADS, h, h)


def _rwkv(p_rwkv, row0, B, L, sh0, s0, consts):
    nvalid = min(L, CHUNK)
    nchunk = L // nvalid
    rb0 = row0 // nvalid
    full = lambda shape: pl.BlockSpec(shape, lambda b, c: (0,) * len(shape))
    in_specs = [
        pl.BlockSpec((nvalid, RWKV_PROJ_PAD), lambda b, c: (rb0 + b * nchunk + c, 0)),
        pl.BlockSpec((1, 1, RWKV_PROJ_PAD), lambda b, c: (b, 0, 0)),
        pl.BlockSpec((1, RWKV_PAIRS, 2 * RWKV_HEAD, 2 * RWKV_HEAD), lambda b, c: (b, 0, 0, 0)),
    ] + [full(x.shape) for x in consts]
    y, s_out = pl.pallas_call(
        functools.partial(_rwkv_kernel, nvalid=nvalid, nchunk=nchunk),
        grid=(B, nchunk),
        in_specs=in_specs,
        out_specs=[
            pl.BlockSpec((nvalid, RWKV_W), lambda b, c: (b * nchunk + c, 0)),
            pl.BlockSpec((1, RWKV_PAIRS, 2 * RWKV_HEAD, 2 * RWKV_HEAD), lambda b, c: (b, 0, 0, 0)),
        ],
        out_shape=[
            jax.ShapeDtypeStruct((B * L, RWKV_W), f32),
            jax.ShapeDtypeStruct((B, RWKV_PAIRS, 2 * RWKV_HEAD, 2 * RWKV_HEAD), f32),
        ],
        scratch_shapes=[
            pltpu.VMEM((1, RWKV_PROJ_PAD), f32),
            pltpu.VMEM((RWKV_PAIRS, 2 * RWKV_HEAD, 2 * RWKV_HEAD), f32),
        ],
        compiler_params=pltpu.CompilerParams(
            dimension_semantics=("arbitrary", "arbitrary"), vmem_limit_bytes=VMEM_LIMIT),
        name=f"rwkv7_{L}",
    )(p_rwkv, sh0, _pair_states(s0), *consts)
    return y, _unpair_states(s_out)


def _outproj_kernel(yr_ref, yw_ref, x_ref, wr_ref, ww_ref, g_ref, x1_ref, h_ref):
    x1 = (x_ref[...] + jnp.dot(yr_ref[...].astype(bf16), wr_ref[...], preferred_element_type=f32)
          + jnp.dot(yw_ref[...].astype(bf16), ww_ref[...], preferred_element_type=f32))
    x1_ref[...] = x1
    ms = jnp.mean(x1 * x1, axis=-1, keepdims=True)
    h_ref[...] = x1 * lax.rsqrt(ms + EPS) * g_ref[...]


def _outproj(y_ret, y_rwkv, x, w_ret_bf, w_rwkv_bf, g, tm):
    T, D = x.shape
    row = lambda shape: pl.BlockSpec(shape, lambda i: (i, 0))
    full = lambda shape: pl.BlockSpec(shape, lambda i: (0, 0))
    return pl.pallas_call(
        _outproj_kernel,
        grid=(T // tm,),
        in_specs=[row((tm, RET_V_W)), row((tm, RWKV_W)), row((tm, D)),
                  full(w_ret_bf.shape), full(w_rwkv_bf.shape), full((1, D))],
        out_specs=[row((tm, D)), row((tm, D))],
        out_shape=[jax.ShapeDtypeStruct((T, D), f32), jax.ShapeDtypeStruct((T, D), f32)],
        compiler_params=pltpu.CompilerParams(
            dimension_semantics=("parallel",), vmem_limit_bytes=VMEM_LIMIT),
        name=f"outproj_{T}",
    )(y_ret, y_rwkv, x, w_ret_bf, w_rwkv_bf, g)


def _topk_multi(s_list, pos_list, vals_refs, idx_refs, lanes_list):
    n = len(s_list)
    big = 3.0e38
    for it in range(PEER_TOPK):
        m = [jnp.max(s, axis=0, keepdims=True) for s in s_list]
        idx = [jnp.min(jnp.where(s_list[q] == m[q], pos_list[q], big), axis=0, keepdims=True) for q in range(n)]
        for q in range(n):
            vals_refs[q][it:it + 1, lanes_list[q]] = m[q]
            idx_refs[q][it:it + 1, lanes_list[q]] = idx[q]
        s_list = [jnp.where(pos_list[q] == idx[q], -jnp.inf, s_list[q]) for q in range(n)]


def _cand_positions():
    K = PEER_TOPK
    pos = [float(j) for j in range(K)]
    for i in range(1, 8):
        pos += [float(i * K + j) if j < K // (i + 1) else -1.0 for j in range(8)]
    pos += [float(i * K) for i in range(8, K)]
    return np.asarray(pos, np.float32)


N_CAND = 16 + 7 * 8 + 8


def _peer_topk_kernel(h_ref, wq_ref, sk_ref, cpos_ref, e_ref, g_ref, v1, i1, v2, i2, tv, tp, *, tm):
    q = jnp.dot(h_ref[...].astype(bf16), wq_ref[...], preferred_element_type=f32)
    K = PEER_TOPK
    nt = tm // LANES
    tiles = [slice(lt * LANES, (lt + 1) * LANES) for lt in range(nt)]
    key_iota = lax.broadcasted_iota(jnp.int32, (N_KEYS, LANES), 0).astype(f32)
    s_list, vrefs, irefs, lanes_list = [], [], [], []
    for lanes in tiles:
        for half, (vr, ir) in enumerate(((v1, i1), (v2, i2))):
            qh = q[lanes, half * PEER_HALF:(half + 1) * PEER_HALF]
            s_list.append(_dot_nt(sk_ref[0, half], qh))
            vrefs.append(vr)
            irefs.append(ir)
            lanes_list.append(lanes)
    _topk_multi(s_list, [key_iota] * len(s_list), vrefs, irefs, lanes_list)

    cpos = cpos_ref[...]
    cands = []
    for lanes in tiles:
        a = v1[:, lanes]
        b = v2[:, lanes]
        blocks = [a[0:1, :] + b]
        blocks += [a[i:i + 1, :] + b[0:8, :] for i in range(1, 8)]
        blocks.append(a[8:K, :] + b[0:1, :])
        cands.append(jnp.where(cpos >= 0.0, jnp.concatenate(blocks, axis=0), -jnp.inf))
    _topk_multi(cands, [cpos] * nt, [tv] * nt, [tp] * nt, tiles)

    for lanes in tiles:
        pos = tp[:, lanes]
        pi = jnp.floor(pos * (1.0 / K))
        pj = pos - pi * K
        e1 = jnp.zeros((K, LANES), f32)
        e2 = jnp.zeros((K, LANES), f32)
        for i in range(K):
            e1 = e1 + jnp.where(pi == float(i), i1[i:i + 1, lanes], 0.0)
            e2 = e2 + jnp.where(pj == float(i), i2[i:i + 1, lanes], 0.0)
        e_ref[:, lanes] = (e1 * float(N_KEYS) + e2).astype(jnp.int32)
        top = tv[:, lanes]
        ex = jnp.exp(top - top[0:1, :])
        g_ref[:, lanes] = ex / jnp.sum(ex, axis=0, keepdims=True)


def _peer_topk(h2, wq_bf, sk_bf, tm):
    T, D = h2.shape
    K = PEER_TOPK
    cpos = jnp.asarray(np.broadcast_to(_cand_positions()[:, None], (N_CAND, LANES)))
    return pl.pallas_call(
        functools.partial(_peer_topk_kernel, tm=tm),
        grid=(T // tm, PEER_HEADS),
        in_specs=[
            pl.BlockSpec((tm, D), lambda i, h: (i, 0)),
            pl.BlockSpec((D, PEER_DK), lambda i, h: (0, h)),
            pl.BlockSpec((1, 2, N_KEYS, PEER_HALF), lambda i, h: (h, 0, 0, 0)),
            pl.BlockSpec((N_CAND, LANES), lambda i, h: (0, 0)),
        ],
        out_specs=[
            pl.BlockSpec((K, tm), lambda i, h: (h, i)),
            pl.BlockSpec((K, tm), lambda i, h: (h, i)),
        ],
        out_shape=[jax.ShapeDtypeStruct((PEER_SLOTS, T), jnp.int32),
                   jax.ShapeDtypeStruct((PEER_SLOTS, T), f32)],
        scratch_shapes=[pltpu.VMEM((K, tm), f32)] * 6,
        compiler_params=pltpu.CompilerParams(
            dimension_semantics=("parallel", "arbitrary"), vmem_limit_bytes=VMEM_LIMIT),
        name=f"peer_topk_{T}",
    )(h2, wq_bf, sk_bf, cpos)


PEER_NBUF = 8
PEER_GROUP = 8
DS = D_MODEL // LANES
EROWS = 2 * DS


def _peer_mix_kernel(idx_ref, idx_next_ref, gate_ref, h_ref, x1_ref, fg_ref, uv_hbm, o_ref, *scratch, tb, nsteps):
    bufs = scratch[:PEER_NBUF]
    z_scr, sem = scratch[PEER_NBUF:]
    S = PEER_SLOTS
    tile = lambda s: slice(s * LANES, (s + 1) * LANES)
    G = PEER_GROUP
    ngroups = tb // G
    step = pl.program_id(0)

    def issue(t, slot, ids=idx_ref):
        for e in range(S):
            pltpu.make_async_copy(uv_hbm.at[ids[t, e]], bufs[slot].at[:, e, :], sem.at[slot]).start()

    def wait(slot):
        pltpu.make_async_copy(bufs[slot], bufs[slot], sem.at[slot]).wait()

    @pl.when(step == 0)
    def _():
        for t in range(PEER_NBUF):
            issue(t, t)

    fg = fg_ref[...]

    def group(t0, first, last):
        gcols = jnp.transpose(gate_ref[pl.ds(t0, G), :])
        for j in range(G):
            t = t0 + j
            wait(j)
            if j == 0 and first:
                pl.when(step > 0)(functools.partial(issue, G - 1, G - 1))
            elif j == 0:
                issue(t + G - 1, G - 1)
            elif not last:
                issue(t + G - 1, j - 1)
            else:
                pl.when(step < nsteps - 1)(functools.partial(issue, j - 1, j - 1, idx_next_ref))
            buf = bufs[j]
            xt = h_ref[pl.ds(t, 1), :]
            x1t = x1_ref[pl.ds(t, 1), :]
            acc = buf[0] * xt[:, tile(0)]
            for s in range(1, DS):
                acc = acc + buf[s] * xt[:, tile(s)]
            a = jnp.sum(acc, axis=1, keepdims=True)
            act = 0.5 * a * (1.0 + lax.erf(a * (2.0 ** -0.5)))
            cw = jnp.broadcast_to(gcols[:, j:j + 1] * act, (S, LANES))
            for s in range(DS):
                z_scr[j:j + 1, tile(s)] = x1t[:, tile(s)] + jnp.sum(buf[DS + s] * cw, axis=0, keepdims=True)
        z = z_scr[...]
        ms = jnp.mean(z * z, axis=-1, keepdims=True)
        o_ref[pl.ds(t0, G), :] = z * lax.rsqrt(ms + EPS) * fg

    group(0, True, False)

    def mid(gi, carry):
        group(pl.multiple_of(gi * G, G), False, False)
        return carry

    lax.fori_loop(1, ngroups - 1, mid, 0)
    group((ngroups - 1) * G, False, True)


def _expert_slabs(expert_u, expert_v):
    E = expert_u.shape[0]
    return jnp.concatenate([expert_u.reshape(E, DS, LANES), expert_v.reshape(E, DS, LANES)], 1)


def _peer_mix(idx, gates, h2, x1, fg, uv, tb):
    T, D = h2.shape
    S = PEER_SLOTS
    assert PEER_GROUP == PEER_NBUF and tb % PEER_GROUP == 0 and tb >= 2 * PEER_GROUP
    rows = pl.BlockSpec((tb, D), lambda i: (i, 0))
    nsteps = T // tb
    G = PEER_GROUP
    return pl.pallas_call(
        functools.partial(_peer_mix_kernel, tb=tb, nsteps=nsteps),
        grid=(nsteps,),
        in_specs=[
            pl.BlockSpec((tb, S), lambda i: (i, 0), memory_space=pltpu.SMEM),
            pl.BlockSpec((G, S), lambda i: (jnp.minimum(i + 1, nsteps - 1) * (tb // G), 0),
                         memory_space=pltpu.SMEM),
            pl.BlockSpec((tb, S), lambda i: (i, 0)),
            rows,
            rows,
            pl.BlockSpec((1, D), lambda i: (0, 0)),
            pl.BlockSpec(memory_space=pl.ANY),
        ],
        out_specs=rows,
        out_shape=jax.ShapeDtypeStruct((T, D), f32),
        scratch_shapes=[pltpu.VMEM((EROWS, S, LANES), f32)] * PEER_NBUF + [
            pltpu.VMEM((PEER_GROUP, D), f32),
            pltpu.SemaphoreType.DMA((PEER_NBUF,)),
        ],
        compiler_params=pltpu.CompilerParams(
            dimension_semantics=("arbitrary",), vmem_limit_bytes=VMEM_LIMIT),
        name=f"peer_mix_{T}",
    )(idx, idx, gates, h2, x1, fg.reshape(1, D), uv)


def _rwkv_consts(mu, w0, w2, a0, a2, g2, k_k, k_a, r_k, lnx_g, lnx_b):
    pad = RWKV_PROJ_PAD - RWKV_PROJ
    mu_p = jnp.pad(mu, (0, pad)).reshape(1, RWKV_PROJ_PAD)
    w2_p = jnp.concatenate([w2, jnp.zeros((A_LORA, RWKV_W), f32)], 0)
    a2_p = jnp.concatenate([jnp.zeros((W_LORA, RWKV_W), f32), a2], 0)
    g2_p = jnp.concatenate([g2, jnp.zeros((2 * LANES - G_LORA, RWKV_W), f32)], 0)
    hd = np.arange(2 * RWKV_HEAD) // RWKV_HEAD
    bd = jnp.asarray((hd[:, None] == hd[None, :]).astype(np.float32), dtype=bf16)
    r1 = lambda x: x.reshape(1, RWKV_W)
    return (mu_p, r1(w0), r1(a0), w2_p, a2_p, g2_p, r1(k_k), r1(k_a), r1(r_k), r1(lnx_g), r1(lnx_b), bd)


def kernel(x_prompt, x_sample, state_ret, state_rwkv, state_rwkv_shift, norm1_g, w_in, ret_gain, rwkv_mu,
           rwkv_w0, rwkv_w2, rwkv_a0, rwkv_a2, rwkv_g2, rwkv_k_k, rwkv_k_a, rwkv_r_k, lnx_g, lnx_b, w_out,
           norm2_g, peer_w_query, peer_sub_keys, peer_u, peer_v, final_g):
    Bp, Lp, D = x_prompt.shape
    Bs, Ls, _ = x_sample.shape
    past_len = 2048
    pad = RWKV_PROJ_PAD - RWKV_PROJ

    l = 0
    w_in_l = w_in[l]
    w_ret = w_in_l[:, :RET_PROJ].astype(bf16)
    w_rwkv = jnp.pad(w_in_l[:, RET_PROJ:], ((0, 0), (0, pad))).astype(bf16)
    g1 = norm1_g[l].reshape(1, D)
    gain = ret_gain[l].reshape(1, RET_V_W)
    consts = _rwkv_consts(rwkv_mu[l], rwkv_w0[l], rwkv_w2[l], rwkv_a0[l], rwkv_a2[l], rwkv_g2[l], rwkv_k_k[l],
                          rwkv_k_a[l], rwkv_r_k[l].reshape(-1), lnx_g[l], lnx_b[l])
    w_out_l = w_out[l].astype(bf16)
    g2 = norm2_g[l].reshape(1, D)
    wq = peer_w_query[l].astype(bf16)
    sk = peer_sub_keys[l].astype(bf16)
    uv = _expert_slabs(peer_u[l], peer_v[l])

    def layer(xg, pos0, s_ret, s_rwkv, shift, ret_block, tm):
        B, L, _ = xg.shape
        x = xg.reshape(B * L, D)
        p_ret = _inproj(x, g1, w_ret, tm, 512)
        p_rwkv = _inproj(x, g1, w_rwkv, tm, 512)
        y_ret, s_ret_new = _retention(p_ret, 0, B, L, ret_block, pos0, s_ret, gain)
        y_rwkv, s_rwkv_new = _rwkv(p_rwkv, 0, B, L, jnp.pad(shift, ((0, 0), (0, 0), (0, pad))), s_rwkv, consts)
        x1, h2 = _outproj(y_ret, y_rwkv, x, w_out_l[:RET_V_W], w_out_l[RET_V_W:], g2, 512)
        e_t, g_t = _peer_topk(h2, wq, sk, 512)
        y = _peer_mix(e_t.T, g_t.T, h2, x1, final_g, uv, 256)
        shift_new = p_rwkv.reshape(B, L, RWKV_PROJ_PAD)[:, -1:, :RWKV_PROJ]
        return y.reshape(B, L, D), s_ret_new[None], s_rwkv_new[None], shift_new[None]

    y_p, sr_p, sw_p, sh_p = layer(x_prompt, 0, jnp.zeros((Bp, RET_HEADS, RET_DK, RET_DV), f32),
                                  jnp.zeros((Bp, RWKV_HEADS, RWKV_HEAD, RWKV_HEAD), f32),
                                  jnp.zeros((Bp, 1, RWKV_PROJ), f32), 256, 1024)
    y_s, sr_s, sw_s, sh_s = layer(x_sample, past_len, state_ret[l], state_rwkv[l], state_rwkv_shift[l], Ls, 1024)
    return (y_p, y_s, sr_p, sw_p, sh_p, sr_s, sw_s, sh_s)
```

```python
import functools
import math

import numpy as np
import jax
import jax.numpy as jnp
from jax import lax
from jax.experimental import pallas as pl
from jax.experimental.pallas import tpu as pltpu

f32 = jnp.float32
bf16 = jnp.bfloat16

D_MODEL = 2048
CHUNK = 64
EPS = 1e-6
RET_HEADS = 4
RET_DK = 128
RET_DV = 256
RET_QK_W = RET_HEADS * RET_DK
RET_V_W = RET_HEADS * RET_DV
RET_PROJ = 2 * RET_QK_W + 2 * RET_V_W
ROPE_BASE = 10000.0
RWKV_HEAD = 64
RWKV_W = D_MODEL // 2
RWKV_HEADS = RWKV_W // RWKV_HEAD
RWKV_PAIRS = RWKV_HEADS // 2
W_LORA = 64
A_LORA = 64
G_LORA = 160
RWKV_PROJ = 3 * RWKV_W + W_LORA + A_LORA + G_LORA
RWKV_PROJ_PAD = 3584
RWKV_GN_EPS = 64e-5
PEER_HEADS = 8
N_KEYS = 128
PEER_DK = 256
PEER_HALF = PEER_DK // 2
PEER_TOPK = 16
PEER_SLOTS = PEER_HEADS * PEER_TOPK

LANES = 128
VMEM_LIMIT = 56 * 1024 * 1024


def _dot(a, b):
    return jnp.dot(a.astype(bf16), b.astype(bf16), preferred_element_type=f32)


def _dot_nt(a, b):
    return lax.dot_general(a.astype(bf16), b.astype(bf16), (((1,), (1,)), ((), ())),
                           preferred_element_type=f32)


def _dot_tn(a, b):
    return lax.dot_general(a.astype(bf16), b.astype(bf16), (((0,), (0,)), ((), ())),
                           preferred_element_type=f32)


def _split2(x):
    hi = x.astype(bf16)
    lo = (x - hi.astype(f32)).astype(bf16)
    return hi, lo


def _split3(x):
    hi = x.astype(bf16)
    r1 = x - hi.astype(f32)
    mid = r1.astype(bf16)
    lo = (r1 - mid.astype(f32)).astype(bf16)
    return hi, mid, lo


def _dot_exact_rhs(a, b_bf):
    hi, mid, lo = _split3(a)
    return (jnp.dot(hi, b_bf, preferred_element_type=f32)
            + jnp.dot(mid, b_bf, preferred_element_type=f32)
            + jnp.dot(lo, b_bf, preferred_element_type=f32))


def _dot_exact_lhs(a_bf, b):
    hi, mid, lo = _split3(b)
    return (jnp.dot(a_bf, hi, preferred_element_type=f32)
            + jnp.dot(a_bf, mid, preferred_element_type=f32)
            + jnp.dot(a_bf, lo, preferred_element_type=f32))


def _dot_hi(a, b):
    ah, al = _split2(a)
    bh, bl = _split2(b)
    return (jnp.dot(ah, bh, preferred_element_type=f32)
            + jnp.dot(ah, bl, preferred_element_type=f32)
            + jnp.dot(al, bh, preferred_element_type=f32))


def _inproj_kernel(x_ref, g_ref, w_ref, o_ref, h_ref):
    @pl.when(pl.program_id(1) == 0)
    def _():
        x = x_ref[...]
        ms = jnp.mean(x * x, axis=-1, keepdims=True)
        h_ref[...] = (x * lax.rsqrt(ms + EPS) * g_ref[...]).astype(bf16)

    o_ref[...] = jnp.dot(h_ref[...], w_ref[...], preferred_element_type=f32)


def _inproj(x, g, w_bf, tm, tn):
    T, D = x.shape
    N = w_bf.shape[1]
    return pl.pallas_call(
        _inproj_kernel,
        grid=(T // tm, N // tn),
        in_specs=[
            pl.BlockSpec((tm, D), lambda i, j: (i, 0)),
            pl.BlockSpec((1, D), lambda i, j: (0, 0)),
            pl.BlockSpec((D, tn), lambda i, j: (0, j)),
        ],
        out_specs=pl.BlockSpec((tm, tn), lambda i, j: (i, j)),
        out_shape=jax.ShapeDtypeStruct((T, N), f32),
        scratch_shapes=[pltpu.VMEM((tm, D), bf16)],
        compiler_params=pltpu.CompilerParams(
            dimension_semantics=("parallel", "arbitrary"), vmem_limit_bytes=VMEM_LIMIT),
        name=f"inproj_{T}x{N}",
    )(x, g, w_bf)


def _ret_kernel(q_ref, k_ref, v_ref, gt_ref, cs_ref, sn_ref, mask_ref, dec_ref, gain_ref, s0_ref,
                y_ref, sout_ref, s_scr, *, nblk):
    c = pl.program_id(1)

    @pl.when(c == 0)
    def _():
        s_scr[...] = s0_ref[0]

    HR = range(RET_HEADS)
    cs = cs_ref[...]
    sn = sn_ref[...]
    half = RET_DK // 2
    ksl = [slice(h * RET_DK, (h + 1) * RET_DK) for h in HR]
    vsl = [slice(h * RET_DV, (h + 1) * RET_DV) for h in HR]
    q = [q_ref[:, ksl[h]] for h in HR]
    k = [k_ref[:, ksl[h]] for h in HR]
    qr = [q[h] * cs + pltpu.roll(q[h], half, 1) * sn for h in HR]
    kr = [(k[h] * cs + pltpu.roll(k[h], half, 1) * sn) * (RET_DK ** -0.5) for h in HR]
    dec = [dec_ref[h] for h in HR]
    v = [v_ref[:, vsl[h]].astype(bf16) for h in HR]
    s = [s_scr[h] for h in HR]
    sc = [_dot_nt(qr[h], kr[h]) * mask_ref[h] for h in HR]
    o = [_dot(sc[h], v[h]) + _dot(qr[h] * dec[h][:, 0:1], s[h]) for h in HR]
    s_new = [dec[h][0:1, 2:3] * s[h] + _dot_tn(kr[h] * dec[h][:, 1:2], v[h]) for h in HR]
    for h in HR:
        s_scr[h] = s_new[h]
        ms = jnp.mean(o[h] * o[h], axis=-1, keepdims=True)
        on = o[h] * lax.rsqrt(ms + EPS) * gain_ref[:, vsl[h]]
        gt = gt_ref[:, vsl[h]]
        y_ref[:, vsl[h]] = gt * jax.nn.sigmoid(gt) * on

    @pl.when(c == nblk - 1)
    def _():
        sout_ref[0] = s_scr[...]


def _ret_tables(L, cb, pos0, chunked):
    half = RET_DK // 2
    inv = ROPE_BASE ** (-jnp.arange(half, dtype=f32) / half)
    pos = (pos0 + jnp.arange(L, dtype=jnp.int32)).astype(f32)
    ang = pos[:, None] * inv[None, :]
    cos = jnp.cos(ang)
    sin = jnp.sin(ang)
    cs = jnp.concatenate([cos, cos], -1)
    sn = jnp.concatenate([-sin, sin], -1)
    log_gamma = jnp.log1p(-jnp.exp2(-5.0 - jnp.arange(RET_HEADS, dtype=f32)))
    idx = jnp.arange(cb, dtype=f32)
    dist = jnp.abs(idx[:, None] - idx[None, :])
    mask = jnp.exp(log_gamma[:, None, None] * dist[None])
    if chunked:
        ci = jnp.arange(cb) // CHUNK
        mask = jnp.where((ci[None, :] <= ci[:, None])[None], mask, 0.0)
    qdec = jnp.exp(log_gamma[:, None] * (idx[None, :] + 1.0))
    kdec = jnp.exp(log_gamma[:, None] * (cb - 1.0 - idx[None, :]))
    gblk = jnp.broadcast_to(jnp.exp(log_gamma * cb)[:, None], (RET_HEADS, cb))
    dec = jnp.zeros((RET_HEADS, cb, LANES), f32)
    dec = dec.at[:, :, 0].set(qdec).at[:, :, 1].set(kdec).at[:, :, 2].set(gblk)
    return cs, sn, mask, dec


def _retention(p_ret, row0, B, L, cb, pos0, s0, gain):
    nblk = L // cb
    rb0 = row0 // cb
    cs, sn, mask, dec = _ret_tables(L, cb, pos0, chunked=L > CHUNK)
    row = lambda b, c: rb0 + b * nblk + c
    y, s_out = pl.pallas_call(
        functools.partial(_ret_kernel, nblk=nblk),
        grid=(B, nblk),
        in_specs=[
            pl.BlockSpec((cb, RET_QK_W), lambda b, c: (row(b, c), 0)),
            pl.BlockSpec((cb, RET_QK_W), lambda b, c: (row(b, c), 1)),
            pl.BlockSpec((cb, RET_V_W), lambda b, c: (row(b, c), 1)),
            pl.BlockSpec((cb, RET_V_W), lambda b, c: (row(b, c), 2)),
            pl.BlockSpec((cb, RET_DK), lambda b, c: (c, 0)),
            pl.BlockSpec((cb, RET_DK), lambda b, c: (c, 0)),
            pl.BlockSpec((RET_HEADS, cb, cb), lambda b, c: (0, 0, 0)),
            pl.BlockSpec((RET_HEADS, cb, LANES), lambda b, c: (0, 0, 0)),
            pl.BlockSpec((1, RET_V_W), lambda b, c: (0, 0)),
            pl.BlockSpec((1, RET_HEADS, RET_DK, RET_DV), lambda b, c: (b, 0, 0, 0)),
        ],
        out_specs=[
            pl.BlockSpec((cb, RET_V_W), lambda b, c: (b * nblk + c, 0)),
            pl.BlockSpec((1, RET_HEADS, RET_DK, RET_DV), lambda b, c: (b, 0, 0, 0)),
        ],
        out_shape=[
            jax.ShapeDtypeStruct((B * L, RET_V_W), f32),
            jax.ShapeDtypeStruct((B, RET_HEADS, RET_DK, RET_DV), f32),
        ],
        scratch_shapes=[pltpu.VMEM((RET_HEADS, RET_DK, RET_DV), f32)],
        compiler_params=pltpu.CompilerParams(
            dimension_semantics=("arbitrary", "arbitrary"), vmem_limit_bytes=VMEM_LIMIT),
        name=f"retention_{L}",
    )(p_ret, p_ret, p_ret, p_ret, cs, sn, mask, dec, gain, s0)
    return y, s_out


def _rwkv_kernel(p_ref, sh0_ref, s0_ref, mu_ref, w0_ref, a0_ref, w2_ref, a2_ref, g2_ref, kk_ref, ka_ref,
                 rk_ref, lg_ref, lb_ref, bd_ref, y_ref, sout_ref, carry, s_scr, *, nvalid, nchunk):
    C = CHUNK
    c = pl.program_id(1)

    @pl.when(c == 0)
    def _():
        carry[...] = sh0_ref[0]
        s_scr[...] = s0_ref[0]

    if nvalid == C:
        pf = p_ref[...]
    else:
        pf = jnp.concatenate([p_ref[...], jnp.zeros((C - nvalid, RWKV_PROJ_PAD), f32)], axis=0)
    rows = lax.broadcasted_iota(jnp.int32, (C, 1), 0)
    prev = jnp.where(rows == 0, carry[...], pltpu.roll(pf, 1, 0))
    pm = pf + (prev - pf) * mu_ref[...]
    carry[...] = pf[nvalid - 1:nvalid, :]

    o1, o2, o3 = RWKV_W, 2 * RWKV_W, 3 * RWKV_W
    r = pm[:, :o1]
    k = pm[:, o1:o2]
    v = pm[:, o2:o3]
    xwa = pm[:, o3:o3 + W_LORA + A_LORA]
    xg = pm[:, o3 + W_LORA + A_LORA:o3 + W_LORA + A_LORA + 2 * LANES]
    z = w0_ref[...] + _dot_hi(jnp.tanh(xwa), w2_ref[...])
    softplus = jnp.maximum(-z, 0.0) + jnp.log1p(jnp.exp(-jnp.abs(z)))
    ew = jnp.exp(-softplus - 0.5)
    a = jax.nn.sigmoid(a0_ref[...] + _dot_hi(xwa, a2_ref[...]))
    g = _dot_hi(jax.nn.sigmoid(xg), g2_ref[...])
    k2 = k * (1.0 + (a - 1.0) * ka_ref[...])
    kkr = k * kk_ref[...]
    if nvalid != C:
        valid = rows < nvalid
        ew = jnp.where(valid, ew, 0.0)
        k2 = jnp.where(valid, k2, 0.0)
        v = jnp.where(valid, v, 0.0)
        a = jnp.where(valid, a, 0.0)

    ti = lax.broadcasted_iota(jnp.int32, (C, C), 0)
    tj = lax.broadcasted_iota(jnp.int32, (C, C), 1)
    tri = jnp.where(tj <= ti, 1.0, 0.0).astype(bf16)
    cum = _dot_exact_lhs(tri, -ew)
    pt = jnp.exp(cum)
    pprev = jnp.exp(cum + ew)
    pinv = jnp.exp(-cum)
    clast = cum[C - 1:C, :]
    pcinv = jnp.exp(clast - cum)
    pc = jnp.exp(clast)
    rt_all = r * pt
    kt_all = k2 * pinv
    kp_all = k2 * pcinv
    rkk = r * k2 * rk_ref[...]

    lane = lax.broadcasted_iota(jnp.int32, (C, 2 * RWKV_HEAD), 1)
    m0 = lane < RWKV_HEAD
    gi = lax.broadcasted_iota(jnp.int32, (4 * C, 4 * C), 0)
    gj = lax.broadcasted_iota(jnp.int32, (4 * C, 4 * C), 1)
    gil = gi % C
    gjl = gj % C
    gmask = gjl < gil + jnp.where(gi < 2 * C, 0, 1)
    ei = lax.broadcasted_iota(jnp.int32, (2 * C, 2 * C), 0)
    ej = lax.broadcasted_iota(jnp.int32, (2 * C, 2 * C), 1)
    eye = jnp.where(ei == ej, 1.0, 0.0)
    bd = bd_ref[...]

    def stack(x):
        return jnp.concatenate([jnp.where(m0, x, 0.0), jnp.where(m0, 0.0, x)], axis=0)

    def gsum(x):
        hi, lo = _split2(x)
        return jnp.dot(hi, bd, preferred_element_type=f32) + jnp.dot(lo, bd, preferred_element_type=f32)

    nsq = int(math.log2(C))
    PR = range(RWKV_PAIRS)
    lss = [slice(p * 2 * RWKV_HEAD, (p + 1) * 2 * RWKV_HEAD) for p in PR]
    kkp = [kkr[:, ls] for ls in lss]
    nrm = [jnp.sqrt(gsum(x * x)) for x in kkp]
    kk = [kkp[p] / jnp.maximum(nrm[p], 1e-12) for p in PR]
    bb = [kk[p] * a[:, lss[p]] for p in PR]
    ar = [jnp.concatenate([stack(-kk[p] * pprev[:, lss[p]]), stack(rt_all[:, lss[p]])], axis=0).astype(bf16)
          for p in PR]
    bk = [jnp.concatenate([stack(bb[p] * pinv[:, lss[p]]), stack(kt_all[:, lss[p]])], axis=0).astype(bf16)
          for p in PR]
    bkp = [jnp.concatenate([stack(bb[p] * pcinv[:, lss[p]]), stack(kp_all[:, lss[p]])], axis=0).astype(bf16)
           for p in PR]
    v_s = [stack(v[:, ls]) for ls in lss]
    s0 = [s_scr[p] for p in PR]
    gm = [jnp.where(gmask, _dot_nt(ar[p], bk[p]), 0.0) for p in PR]
    arh = [_dot_nt(ar[p], s0[p]) for p in PR]
    rhs = [arh[p][:2 * C] + _dot(gm[p][:2 * C, 2 * C:], v_s[p]) for p in PR]
    rbk = [gm[p][2 * C:, :].astype(bf16) for p in PR]
    npow = [gm[p][:2 * C, :2 * C] for p in PR]
    tinv = [eye + npow[p] for p in PR]
    for i in range(1, nsq):
        npow = [_dot(n, n) for n in npow]
        tinv = [tinv[p] + _dot(npow[p], tinv[p]) for p in PR]
    u_s = [_dot(tinv[p], rhs[p]) for p in PR]
    uv = [jnp.concatenate([u_s[p], v_s[p]], axis=0).astype(bf16) for p in PR]
    y_s = [arh[p][2 * C:] + _dot(rbk[p], uv[p]) for p in PR]
    for p in PR:
        s_scr[p] = s0[p] * pc[:, lss[p]] + _dot_tn(uv[p], bkp[p])
    yp = [y[:C] + y[C:] for y in y_s]
    mean = [gsum(y) * (1.0 / RWKV_HEAD) for y in yp]
    yc = [yp[p] - mean[p] for p in PR]
    var = [gsum(y * y) * (1.0 / RWKV_HEAD) for y in yc]
    bonus = [gsum(rkk[:, ls]) for ls in lss]
    for p in PR:
        ls = lss[p]
        yn = yc[p] * lax.rsqrt(var[p] + RWKV_GN_EPS) * lg_ref[:, ls] + lb_ref[:, ls]
        yo = (yn + bonus[p] * v[:, ls]) * g[:, ls]
        y_ref[:, ls] = yo[:nvalid]

    @pl.when(c == nchunk - 1)
    def _():
        sout_ref[0] = s_scr[...]


def _pair_states(s):
    B = s.shape[0]
    sp = s.reshape(B, RWKV_PAIRS, 2, RWKV_HEAD, RWKV_HEAD)
    z = jnp.zeros_like(sp[:, :, 0])
    top = jnp.concatenate([sp[:, :, 0], z], -1)
    bot = jnp.concatenate([z, sp[:, :, 1]], -1)
    return jnp.concatenate([top, bot], -2)


def _unpair_states(sp):
    B = sp.shape[0]
    h = RWKV_HEAD
    return jnp.stack([sp[:, :, :h, :h], sp[:, :, h:, h:]], 2).reshape(B, RWKV_HEADS, h, h)


def _rwkv(p_rwkv, row0, B, L, sh0, s0, consts):
    nvalid = min(L, CHUNK)
    nchunk = L // nvalid
    rb0 = row0 // nvalid
    full = lambda shape: pl.BlockSpec(shape, lambda b, c: (0,) * len(shape))
    in_specs = [
        pl.BlockSpec((nvalid, RWKV_PROJ_PAD), lambda b, c: (rb0 + b * nchunk + c, 0)),
        pl.BlockSpec((1, 1, RWKV_PROJ_PAD), lambda b, c: (b, 0, 0)),
        pl.BlockSpec((1, RWKV_PAIRS, 2 * RWKV_HEAD, 2 * RWKV_HEAD), lambda b, c: (b, 0, 0, 0)),
    ] + [full(x.shape) for x in consts]
    y, s_out = pl.pallas_call(
        functools.partial(_rwkv_kernel, nvalid=nvalid, nchunk=nchunk),
        grid=(B, nchunk),
        in_specs=in_specs,
        out_specs=[
            pl.BlockSpec((nvalid, RWKV_W), lambda b, c: (b * nchunk + c, 0)),
            pl.BlockSpec((1, RWKV_PAIRS, 2 * RWKV_HEAD, 2 * RWKV_HEAD), lambda b, c: (b, 0, 0, 0)),
        ],
        out_shape=[
            jax.ShapeDtypeStruct((B * L, RWKV_W), f32),
            jax.ShapeDtypeStruct((B, RWKV_PAIRS, 2 * RWKV_HEAD, 2 * RWKV_HEAD), f32),
        ],
        scratch_shapes=[
            pltpu.VMEM((1, RWKV_PROJ_PAD), f32),
            pltpu.VMEM((RWKV_PAIRS, 2 * RWKV_HEAD, 2 * RWKV_HEAD), f32),
        ],
        compiler_params=pltpu.CompilerParams(
            dimension_semantics=("arbitrary", "arbitrary"), vmem_limit_bytes=VMEM_LIMIT),
        name=f"rwkv7_{L}",
    )(p_rwkv, sh0, _pair_states(s0), *consts)
    return y, _unpair_states(s_out)


def _outproj_kernel(yr_ref, yw_ref, x_ref, wr_ref, ww_ref, g_ref, x1_ref, h_ref):
    x1 = (x_ref[...] + jnp.dot(yr_ref[...].astype(bf16), wr_ref[...], preferred_element_type=f32)
          + jnp.dot(yw_ref[...].astype(bf16), ww_ref[...], preferred_element_type=f32))
    x1_ref[...] = x1
    ms = jnp.mean(x1 * x1, axis=-1, keepdims=True)
    h_ref[...] = x1 * lax.rsqrt(ms + EPS) * g_ref[...]


def _outproj(y_ret, y_rwkv, x, w_ret_bf, w_rwkv_bf, g, tm):
    T, D = x.shape
    row = lambda shape: pl.BlockSpec(shape, lambda i: (i, 0))
    full = lambda shape: pl.BlockSpec(shape, lambda i: (0, 0))
    return pl.pallas_call(
        _outproj_kernel,
        grid=(T // tm,),
        in_specs=[row((tm, RET_V_W)), row((tm, RWKV_W)), row((tm, D)),
                  full(w_ret_bf.shape), full(w_rwkv_bf.shape), full((1, D))],
        out_specs=[row((tm, D)), row((tm, D))],
        out_shape=[jax.ShapeDtypeStruct((T, D), f32), jax.ShapeDtypeStruct((T, D), f32)],
        compiler_params=pltpu.CompilerParams(
            dimension_semantics=("parallel",), vmem_limit_bytes=VMEM_LIMIT),
        name=f"outproj_{T}",
    )(y_ret, y_rwkv, x, w_ret_bf, w_rwkv_bf, g)


def _topk_multi(s_list, pos_list, vals_refs, idx_refs, lanes_list):
    n = len(s_list)
    big = 3.0e38
    for it in range(PEER_TOPK):
        m = [jnp.max(s, axis=0, keepdims=True) for s in s_list]
        idx = [jnp.min(jnp.where(s_list[q] == m[q], pos_list[q], big), axis=0, keepdims=True) for q in range(n)]
        for q in range(n):
            vals_refs[q][it:it + 1, lanes_list[q]] = m[q]
            idx_refs[q][it:it + 1, lanes_list[q]] = idx[q]
        s_list = [jnp.where(pos_list[q] == idx[q], -jnp.inf, s_list[q]) for q in range(n)]


def _cand_positions():
    K = PEER_TOPK
    pos = [float(j) for j in range(K)]
    for i in range(1, 8):
        pos += [float(i * K + j) if j < K // (i + 1) else -1.0 for j in range(8)]
    pos += [float(i * K) for i in range(8, K)]
    return np.asarray(pos, np.float32)


N_CAND = 16 + 7 * 8 + 8


def _peer_topk_kernel(h_ref, wq_ref, sk_ref, cpos_ref, e_ref, g_ref, v1, i1, v2, i2, tv, tp, *, tm):
    q = jnp.dot(h_ref[...].astype(bf16), wq_ref[...], preferred_element_type=f32)
    K = PEER_TOPK
    nt = tm // LANES
    tiles = [slice(lt * LANES, (lt + 1) * LANES) for lt in range(nt)]
    key_iota = lax.broadcasted_iota(jnp.int32, (N_KEYS, LANES), 0).astype(f32)
    s_list, vrefs, irefs, lanes_list = [], [], [], []
    for lanes in tiles:
        for half, (vr, ir) in enumerate(((v1, i1), (v2, i2))):
            qh = q[lanes, half * PEER_HALF:(half + 1) * PEER_HALF]
            s_list.append(_dot_nt(sk_ref[0, half], qh))
            vrefs.append(vr)
            irefs.append(ir)
            lanes_list.append(lanes)
    _topk_multi(s_list, [key_iota] * len(s_list), vrefs, irefs, lanes_list)

    cpos = cpos_ref[...]
    cands = []
    for lanes in tiles:
        a = v1[:, lanes]
        b = v2[:, lanes]
        blocks = [a[0:1, :] + b]
        blocks += [a[i:i + 1, :] + b[0:8, :] for i in range(1, 8)]
        blocks.append(a[8:K, :] + b[0:1, :])
        cands.append(jnp.where(cpos >= 0.0, jnp.concatenate(blocks, axis=0), -jnp.inf))
    _topk_multi(cands, [cpos] * nt, [tv] * nt, [tp] * nt, tiles)

    for lanes in tiles:
        pos = tp[:, lanes]
        pi = jnp.floor(pos * (1.0 / K))
        pj = pos - pi * K
        e1 = jnp.zeros((K, LANES), f32)
        e2 = jnp.zeros((K, LANES), f32)
        for i in range(K):
            e1 = e1 + jnp.where(pi == float(i), i1[i:i + 1, lanes], 0.0)
            e2 = e2 + jnp.where(pj == float(i), i2[i:i + 1, lanes], 0.0)
        e_ref[:, lanes] = (e1 * float(N_KEYS) + e2).astype(jnp.int32)
        top = tv[:, lanes]
        ex = jnp.exp(top - top[0:1, :])
        g_ref[:, lanes] = ex / jnp.sum(ex, axis=0, keepdims=True)


def _peer_topk(h2, wq_bf, sk_bf, tm):
    T, D = h2.shape
    K = PEER_TOPK
    cpos = jnp.asarray(np.broadcast_to(_cand_positions()[:, None], (N_CAND, LANES)))
    return pl.pallas_call(
        functools.partial(_peer_topk_kernel, tm=tm),
        grid=(T // tm, PEER_HEADS),
        in_specs=[
            pl.BlockSpec((tm, D), lambda i, h: (i, 0)),
            pl.BlockSpec((D, PEER_DK), lambda i, h: (0, h)),
            pl.BlockSpec((1, 2, N_KEYS, PEER_HALF), lambda i, h: (h, 0, 0, 0)),
            pl.BlockSpec((N_CAND, LANES), lambda i, h: (0, 0)),
        ],
        out_specs=[
            pl.BlockSpec((K, tm), lambda i, h: (h, i)),
            pl.BlockSpec((K, tm), lambda i, h: (h, i)),
        ],
        out_shape=[jax.ShapeDtypeStruct((PEER_SLOTS, T), jnp.int32),
                   jax.ShapeDtypeStruct((PEER_SLOTS, T), f32)],
        scratch_shapes=[pltpu.VMEM((K, tm), f32)] * 6,
        compiler_params=pltpu.CompilerParams(
            dimension_semantics=("parallel", "arbitrary"), vmem_limit_bytes=VMEM_LIMIT),
        name=f"peer_topk_{T}",
    )(h2, wq_bf, sk_bf, cpos)


PEER_NBUF = 8
PEER_GROUP = 8
SUB = 8
DS = D_MODEL // LANES
EROWS = 2 * DS


_ROWSUM8_ORDER = (0, 4, 2, 6, 1, 5, 3, 7)


def _rowsum8(xs, sub_id):
    up = lambda x, k: pltpu.roll(x, SUB - k, 0)
    down = lambda x, k: pltpu.roll(x, k, 0)
    p = [xs[i] for i in _ROWSUM8_ORDER]
    lo4 = sub_id < 4
    t = [jnp.where(lo4, p[2 * k], p[2 * k + 1]) + pltpu.roll(jnp.where(lo4, p[2 * k + 1], p[2 * k]), 4, 0)
         for k in range(4)]
    lo2 = (sub_id % 4) < 2
    z = [jnp.where(lo2, t[2 * k] + up(t[2 * k], 2), t[2 * k + 1] + down(t[2 * k + 1], 2)) for k in range(2)]
    lo1 = (sub_id % 2) < 1
    return jnp.where(lo1, z[0] + up(z[0], 1), z[1] + down(z[1], 1))


def _peer_mix_kernel(idx_ref, idx_next_ref, gate_ref, h_ref, x1_ref, fg_ref, uv_hbm, o_ref, *scratch, tb, nsteps):
    bufs = scratch[:PEER_NBUF]
    z_scr, sem = scratch[PEER_NBUF:]
    S = PEER_SLOTS
    tile = lambda s: slice(s * LANES, (s + 1) * LANES)
    G = PEER_GROUP
    ngroups = tb // G
    step = pl.program_id(0)

    def issue(t, slot, ids=idx_ref):
        for e in range(S):
            pltpu.make_async_copy(uv_hbm.at[ids[t, e]], bufs[slot].at[e], sem.at[slot]).start(priority=e % 2)

    def wait(slot):
        pltpu.make_async_copy(bufs[slot], bufs[slot], sem.at[slot]).wait()

    @pl.when(step == 0)
    def _():
        for t in range(PEER_NBUF):
            issue(t, t)

    fg = fg_ref[...]

    def group(t0, first, last):
        gcols = jnp.transpose(gate_ref[pl.ds(t0, G), :])
        gcol_id = lax.broadcasted_iota(jnp.int32, (S, G), 1)
        sub_id = lax.broadcasted_iota(jnp.int32, (SUB, LANES), 0)
        for j in range(G):
            t = t0 + j
            wait(j)
            if j == 0 and first:
                pl.when(step > 0)(functools.partial(issue, G - 1, G - 1))
            elif j == 0:
                issue(t + G - 1, G - 1)
            elif not last:
                issue(t + G - 1, j - 1)
            else:
                pl.when(step < nsteps - 1)(functools.partial(issue, j - 1, j - 1, idx_next_ref))
            buf = bufs[j]
            x3 = h_ref[t]
            x1t = x1_ref[pl.ds(t, 1), :]
            x_lo, x_hi = x3[0:SUB], x3[SUB:DS]
            parts = []
            for g in range(S // SUB):
                prods = []
                for i in range(SUB):
                    ue = buf[g * SUB + i, 0:DS, :].astype(f32)
                    prods.append(ue[0:SUB] * x_lo + ue[SUB:DS] * x_hi)
                parts.append(_rowsum8(prods, sub_id))
            a = jnp.sum(jnp.concatenate(parts, axis=0), axis=1, keepdims=True)
            act = 0.5 * a * (1.0 + lax.erf(a * (2.0 ** -0.5)))
            gate = jnp.sum(jnp.where(gcol_id == j, gcols, 0.0), axis=1, keepdims=True)
            cw = jnp.broadcast_to(gate * act, (S, LANES))
            accs =[jnp.zeros((SUB, LANES), f32) for _ in range(4)]
            for e in range(S):
                ve = buf[e, DS:EROWS, :].astype(f32)
                c = jnp.broadcast_to(cw[e:e + 1, :], (SUB, LANES))
                q = 2 * (e % 2)
                accs[q] = accs[q] + ve[0:SUB] * c
                accs[q + 1] = accs[q + 1] + ve[SUB:DS] * c
            out3 = jnp.concatenate([accs[0] + accs[2], accs[1] + accs[3]], axis=0)
            for s in range(DS):
                z_scr[j:j + 1, tile(s)] = x1t[:, tile(s)] + out3[s:s + 1, :]
        z = z_scr[...]
        ms = jnp.mean(z * z, axis=-1, keepdims=True)
        o_ref[pl.ds(t0, G), :] = z * lax.rsqrt(ms + EPS) * fg

    group(0, True, False)

    def mid(gi, carry):
        group(pl.multiple_of(gi * G, G), False, False)
        return carry

    lax.fori_loop(1, ngroups - 1, mid, 0)
    group((ngroups - 1) * G, False, True)


def _expert_slabs(expert_u, expert_v):
    E = expert_u.shape[0]
    return jnp.concatenate([expert_u.astype(bf16).reshape(E, DS, LANES),
                            expert_v.astype(bf16).reshape(E, DS, LANES)], 1)


def _peer_mix(idx, gates, h2, x1, fg, uv, tb):
    T, D = h2.shape
    S = PEER_SLOTS
    assert PEER_GROUP == PEER_NBUF and tb % PEER_GROUP == 0 and tb >= 2 * PEER_GROUP
    rows = pl.BlockSpec((tb, D), lambda i: (i, 0))
    nsteps = T // tb
    G = PEER_GROUP
    return pl.pallas_call(
        functools.partial(_peer_mix_kernel, tb=tb, nsteps=nsteps),
        grid=(nsteps,),
        in_specs=[
            pl.BlockSpec((tb, S), lambda i: (i, 0), memory_space=pltpu.SMEM),
            pl.BlockSpec((G, S), lambda i: (jnp.minimum(i + 1, nsteps - 1) * (tb // G), 0),
                         memory_space=pltpu.SMEM),
            pl.BlockSpec((tb, S), lambda i: (i, 0)),
            pl.BlockSpec((tb, DS, LANES), lambda i: (i, 0, 0)),
            rows,
            pl.BlockSpec((1, D), lambda i: (0, 0)),
            pl.BlockSpec(memory_space=pl.ANY),
        ],
        out_specs=rows,
        out_shape=jax.ShapeDtypeStruct((T, D), f32),
        scratch_shapes=[pltpu.VMEM((S, EROWS, LANES), bf16)] * PEER_NBUF + [
            pltpu.VMEM((PEER_GROUP, D), f32),
            pltpu.SemaphoreType.DMA((PEER_NBUF,)),
        ],
        compiler_params=pltpu.CompilerParams(
            dimension_semantics=("arbitrary",), vmem_limit_bytes=VMEM_LIMIT),
        name=f"peer_mix_{T}",
    )(idx, idx, gates, h2.reshape(T, DS, LANES), x1, fg.reshape(1, D), uv)


def _rwkv_consts(mu, w0, w2, a0, a2, g2, k_k, k_a, r_k, lnx_g, lnx_b):
    pad = RWKV_PROJ_PAD - RWKV_PROJ
    mu_p = jnp.pad(mu, (0, pad)).reshape(1, RWKV_PROJ_PAD)
    w2_p = jnp.concatenate([w2, jnp.zeros((A_LORA, RWKV_W), f32)], 0)
    a2_p = jnp.concatenate([jnp.zeros((W_LORA, RWKV_W), f32), a2], 0)
    g2_p = jnp.concatenate([g2, jnp.zeros((2 * LANES - G_LORA, RWKV_W), f32)], 0)
    hd = np.arange(2 * RWKV_HEAD) // RWKV_HEAD
    bd = jnp.asarray((hd[:, None] == hd[None, :]).astype(np.float32), dtype=bf16)
    r1 = lambda x: x.reshape(1, RWKV_W)
    return (mu_p, r1(w0), r1(a0), w2_p, a2_p, g2_p, r1(k_k), r1(k_a), r1(r_k), r1(lnx_g), r1(lnx_b), bd)


def kernel(x_prompt, x_sample, state_ret, state_rwkv, state_rwkv_shift, norm1_g, w_in, ret_gain, rwkv_mu,
           rwkv_w0, rwkv_w2, rwkv_a0, rwkv_a2, rwkv_g2, rwkv_k_k, rwkv_k_a, rwkv_r_k, lnx_g, lnx_b, w_out,
           norm2_g, peer_w_query, peer_sub_keys, peer_u, peer_v, final_g):
    Bp, Lp, D = x_prompt.shape
    Bs, Ls, _ = x_sample.shape
    past_len = 2048
    pad = RWKV_PROJ_PAD - RWKV_PROJ

    l = 0
    w_in_l = w_in[l]
    w_ret = w_in_l[:, :RET_PROJ].astype(bf16)
    w_rwkv = jnp.pad(w_in_l[:, RET_PROJ:], ((0, 0), (0, pad))).astype(bf16)
    g1 = norm1_g[l].reshape(1, D)
    gain = ret_gain[l].reshape(1, RET_V_W)
    consts = _rwkv_consts(rwkv_mu[l], rwkv_w0[l], rwkv_w2[l], rwkv_a0[l], rwkv_a2[l], rwkv_g2[l], rwkv_k_k[l],
                          rwkv_k_a[l], rwkv_r_k[l].reshape(-1), lnx_g[l], lnx_b[l])
    w_out_l = w_out[l].astype(bf16)
    g2 = norm2_g[l].reshape(1, D)
    wq = peer_w_query[l].astype(bf16)
    sk = peer_sub_keys[l].astype(bf16)
    uv = _expert_slabs(peer_u[l], peer_v[l])

    def layer(xg, pos0, s_ret, s_rwkv, shift, ret_block, tm):
        B, L, _ = xg.shape
        x = xg.reshape(B * L, D)
        p_ret = _inproj(x, g1, w_ret, tm, 512)
        p_rwkv = _inproj(x, g1, w_rwkv, tm, 512)
        y_ret, s_ret_new = _retention(p_ret, 0, B, L, ret_block, pos0, s_ret, gain)
        y_rwkv, s_rwkv_new = _rwkv(p_rwkv, 0, B, L, jnp.pad(shift, ((0, 0), (0, 0), (0, pad))), s_rwkv, consts)
        x1, h2 = _outproj(y_ret, y_rwkv, x, w_out_l[:RET_V_W], w_out_l[RET_V_W:], g2, 512)
        e_t, g_t = _peer_topk(h2, wq, sk, 512)
        y = _peer_mix(e_t.T, g_t.T, h2, x1, final_g, uv, 64)
        shift_new = p_rwkv.reshape(B, L, RWKV_PROJ_PAD)[:, -1:, :RWKV_PROJ]
        return y.reshape(B, L, D), s_ret_new[None], s_rwkv_new[None], shift_new[None]

    y_p, sr_p, sw_p, sh_p = layer(x_prompt, 0, jnp.zeros((Bp, RET_HEADS, RET_DK, RET_DV), f32),
                                  jnp.zeros((Bp, RWKV_HEADS, RWKV_HEAD, RWKV_HEAD), f32),
                                  jnp.zeros((Bp, 1, RWKV_PROJ), f32), 256, 1024)
    y_s, sr_s, sw_s, sh_s = layer(x_sample, past_len, state_ret[l], state_rwkv[l], state_rwkv_shift[l], Ls, 1024)
    return (y_p, y_s, sr_p, sw_p, sh_p, sr_s, sw_s, sh_s)
```

```python
import functools
import math

import numpy as np
import jax
import jax.numpy as jnp
from jax import lax
from jax.experimental import pallas as pl
from jax.experimental.pallas import tpu as pltpu

f32 = jnp.float32
bf16 = jnp.bfloat16

D_MODEL = 2048
CHUNK = 64
EPS = 1e-6
RET_HEADS = 4
RET_DK = 128
RET_DV = 256
RET_QK_W = RET_HEADS * RET_DK
RET_V_W = RET_HEADS * RET_DV
RET_PROJ = 2 * RET_QK_W + 2 * RET_V_W
ROPE_BASE = 10000.0
RWKV_HEAD = 64
RWKV_W = D_MODEL // 2
RWKV_HEADS = RWKV_W // RWKV_HEAD
RWKV_PAIRS = RWKV_HEADS // 2
W_LORA = 64
A_LORA = 64
G_LORA = 160
RWKV_PROJ = 3 * RWKV_W + W_LORA + A_LORA + G_LORA
RWKV_PROJ_PAD = 3584
RWKV_GN_EPS = 64e-5
PEER_HEADS = 8
N_KEYS = 128
PEER_DK = 256
PEER_HALF = PEER_DK // 2
PEER_TOPK = 16
PEER_SLOTS = PEER_HEADS * PEER_TOPK

LANES = 128
VMEM_LIMIT = 56 * 1024 * 1024


def _dot(a, b):
    return jnp.dot(a.astype(bf16), b.astype(bf16), preferred_element_type=f32)


def _dot_nt(a, b):
    return lax.dot_general(a.astype(bf16), b.astype(bf16), (((1,), (1,)), ((), ())),
                           preferred_element_type=f32)


def _dot_tn(a, b):
    return lax.dot_general(a.astype(bf16), b.astype(bf16), (((0,), (0,)), ((), ())),
                           preferred_element_type=f32)


def _split2(x):
    hi = x.astype(bf16)
    lo = (x - hi.astype(f32)).astype(bf16)
    return hi, lo


def _split3(x):
    hi = x.astype(bf16)
    r1 = x - hi.astype(f32)
    mid = r1.astype(bf16)
    lo = (r1 - mid.astype(f32)).astype(bf16)
    return hi, mid, lo


def _dot_exact_rhs(a, b_bf):
    hi, mid, lo = _split3(a)
    return (jnp.dot(hi, b_bf, preferred_element_type=f32)
            + jnp.dot(mid, b_bf, preferred_element_type=f32)
            + jnp.dot(lo, b_bf, preferred_element_type=f32))


def _dot_exact_lhs(a_bf, b):
    hi, mid, lo = _split3(b)
    return (jnp.dot(a_bf, hi, preferred_element_type=f32)
            + jnp.dot(a_bf, mid, preferred_element_type=f32)
            + jnp.dot(a_bf, lo, preferred_element_type=f32))


def _dot_hi(a, b):
    ah, al = _split2(a)
    bh, bl = _split2(b)
    return (jnp.dot(ah, bh, preferred_element_type=f32)
            + jnp.dot(ah, bl, preferred_element_type=f32)
            + jnp.dot(al, bh, preferred_element_type=f32))


def _inproj_kernel(x_ref, g_ref, w_ref, o_ref, h_ref):
    @pl.when(pl.program_id(1) == 0)
    def _():
        x = x_ref[...]
        ms = jnp.mean(x * x, axis=-1, keepdims=True)
        h_ref[...] = (x * lax.rsqrt(ms + EPS) * g_ref[...]).astype(bf16)

    o_ref[...] = jnp.dot(h_ref[...], w_ref[...], preferred_element_type=f32)


def _inproj(x, g, w_bf, tm, tn):
    T, D = x.shape
    N = w_bf.shape[1]
    return pl.pallas_call(
        _inproj_kernel,
        grid=(T // tm, N // tn),
        in_specs=[
            pl.BlockSpec((tm, D), lambda i, j: (i, 0)),
            pl.BlockSpec((1, D), lambda i, j: (0, 0)),
            pl.BlockSpec((D, tn), lambda i, j: (0, j)),
        ],
        out_specs=pl.BlockSpec((tm, tn), lambda i, j: (i, j)),
        out_shape=jax.ShapeDtypeStruct((T, N), f32),
        scratch_shapes=[pltpu.VMEM((tm, D), bf16)],
        compiler_params=pltpu.CompilerParams(
            dimension_semantics=("parallel", "arbitrary"), vmem_limit_bytes=VMEM_LIMIT),
        name=f"inproj_{T}x{N}",
    )(x, g, w_bf)


def _ret_kernel(q_ref, k_ref, v_ref, gt_ref, cs_ref, sn_ref, mask_ref, dec_ref, gain_ref, s0_ref,
                y_ref, sout_ref, s_scr, *, nblk):
    c = pl.program_id(1)

    @pl.when(c == 0)
    def _():
        s_scr[...] = s0_ref[0]

    HR = range(RET_HEADS)
    cs = cs_ref[...]
    sn = sn_ref[...]
    half = RET_DK // 2
    ksl = [slice(h * RET_DK, (h + 1) * RET_DK) for h in HR]
    vsl = [slice(h * RET_DV, (h + 1) * RET_DV) for h in HR]
    q = [q_ref[:, ksl[h]] for h in HR]
    k = [k_ref[:, ksl[h]] for h in HR]
    qr = [q[h] * cs + pltpu.roll(q[h], half, 1) * sn for h in HR]
    kr = [(k[h] * cs + pltpu.roll(k[h], half, 1) * sn) * (RET_DK ** -0.5) for h in HR]
    dec = [dec_ref[h] for h in HR]
    v = [v_ref[:, vsl[h]].astype(bf16) for h in HR]
    s = [s_scr[h] for h in HR]
    sc = [_dot_nt(qr[h], kr[h]) * mask_ref[h] for h in HR]
    o = [_dot(sc[h], v[h]) + _dot(qr[h] * dec[h][:, 0:1], s[h]) for h in HR]
    s_new = [dec[h][0:1, 2:3] * s[h] + _dot_tn(kr[h] * dec[h][:, 1:2], v[h]) for h in HR]
    for h in HR:
        s_scr[h] = s_new[h]
        ms = jnp.mean(o[h] * o[h], axis=-1, keepdims=True)
        on = o[h] * lax.rsqrt(ms + EPS) * gain_ref[:, vsl[h]]
        gt = gt_ref[:, vsl[h]]
        y_ref[:, vsl[h]] = gt * jax.nn.sigmoid(gt) * on

    @pl.when(c == nblk - 1)
    def _():
        sout_ref[0] = s_scr[...]


def _ret_tables(L, cb, pos0, chunked):
    half = RET_DK // 2
    inv = ROPE_BASE ** (-jnp.arange(half, dtype=f32) / half)
    pos = (pos0 + jnp.arange(L, dtype=jnp.int32)).astype(f32)
    ang = pos[:, None] * inv[None, :]
    cos = jnp.cos(ang)
    sin = jnp.sin(ang)
    cs = jnp.concatenate([cos, cos], -1)
    sn = jnp.concatenate([-sin, sin], -1)
    log_gamma = jnp.log1p(-jnp.exp2(-5.0 - jnp.arange(RET_HEADS, dtype=f32)))
    idx = jnp.arange(cb, dtype=f32)
    dist = jnp.abs(idx[:, None] - idx[None, :])
    mask = jnp.exp(log_gamma[:, None, None] * dist[None])
    if chunked:
        ci = jnp.arange(cb) // CHUNK
        mask = jnp.where((ci[None, :] <= ci[:, None])[None], mask, 0.0)
    qdec = jnp.exp(log_gamma[:, None] * (idx[None, :] + 1.0))
    kdec = jnp.exp(log_gamma[:, None] * (cb - 1.0 - idx[None, :]))
    gblk = jnp.broadcast_to(jnp.exp(log_gamma * cb)[:, None], (RET_HEADS, cb))
    dec = jnp.zeros((RET_HEADS, cb, LANES), f32)
    dec = dec.at[:, :, 0].set(qdec).at[:, :, 1].set(kdec).at[:, :, 2].set(gblk)
    return cs, sn, mask, dec


def _retention(p_ret, row0, B, L, cb, pos0, s0, gain):
    nblk = L // cb
    rb0 = row0 // cb
    cs, sn, mask, dec = _ret_tables(L, cb, pos0, chunked=L > CHUNK)
    row = lambda b, c: rb0 + b * nblk + c
    y, s_out = pl.pallas_call(
        functools.partial(_ret_kernel, nblk=nblk),
        grid=(B, nblk),
        in_specs=[
            pl.BlockSpec((cb, RET_QK_W), lambda b, c: (row(b, c), 0)),
            pl.BlockSpec((cb, RET_QK_W), lambda b, c: (row(b, c), 1)),
            pl.BlockSpec((cb, RET_V_W), lambda b, c: (row(b, c), 1)),
            pl.BlockSpec((cb, RET_V_W), lambda b, c: (row(b, c), 2)),
            pl.BlockSpec((cb, RET_DK), lambda b, c: (c, 0)),
            pl.BlockSpec((cb, RET_DK), lambda b, c: (c, 0)),
            pl.BlockSpec((RET_HEADS, cb, cb), lambda b, c: (0, 0, 0)),
            pl.BlockSpec((RET_HEADS, cb, LANES), lambda b, c: (0, 0, 0)),
            pl.BlockSpec((1, RET_V_W), lambda b, c: (0, 0)),
            pl.BlockSpec((1, RET_HEADS, RET_DK, RET_DV), lambda b, c: (b, 0, 0, 0)),
        ],
        out_specs=[
            pl.BlockSpec((cb, RET_V_W), lambda b, c: (b * nblk + c, 0)),
            pl.BlockSpec((1, RET_HEADS, RET_DK, RET_DV), lambda b, c: (b, 0, 0, 0)),
        ],
        out_shape=[
            jax.ShapeDtypeStruct((B * L, RET_V_W), f32),
            jax.ShapeDtypeStruct((B, RET_HEADS, RET_DK, RET_DV), f32),
        ],
        scratch_shapes=[pltpu.VMEM((RET_HEADS, RET_DK, RET_DV), f32)],
        compiler_params=pltpu.CompilerParams(
            dimension_semantics=("arbitrary", "arbitrary"), vmem_limit_bytes=VMEM_LIMIT),
        name=f"retention_{L}",
    )(p_ret, p_ret, p_ret, p_ret, cs, sn, mask, dec, gain, s0)
    return y, s_out


def _rwkv_kernel(p_ref, sh0_ref, s0_ref, mu_ref, w0_ref, a0_ref, w2_ref, a2_ref, g2_ref, kk_ref, ka_ref,
                 rk_ref, lg_ref, lb_ref, bd_ref, y_ref, sout_ref, carry, s_scr, *, nvalid, nchunk):
    C = CHUNK
    c = pl.program_id(1)

    @pl.when(c == 0)
    def _():
        carry[...] = sh0_ref[0]
        s_scr[...] = s0_ref[0]

    if nvalid == C:
        pf = p_ref[...]
    else:
        pf = jnp.concatenate([p_ref[...], jnp.zeros((C - nvalid, RWKV_PROJ_PAD), f32)], axis=0)
    rows = lax.broadcasted_iota(jnp.int32, (C, 1), 0)
    prev = jnp.where(rows == 0, carry[...], pltpu.roll(pf, 1, 0))
    pm = pf + (prev - pf) * mu_ref[...]
    carry[...] = pf[nvalid - 1:nvalid, :]

    o1, o2, o3 = RWKV_W, 2 * RWKV_W, 3 * RWKV_W
    r = pm[:, :o1]
    k = pm[:, o1:o2]
    v = pm[:, o2:o3]
    xwa = pm[:, o3:o3 + W_LORA + A_LORA]
    xg = pm[:, o3 + W_LORA + A_LORA:o3 + W_LORA + A_LORA + 2 * LANES]
    z = w0_ref[...] + _dot_hi(jnp.tanh(xwa), w2_ref[...])
    softplus = jnp.maximum(-z, 0.0) + jnp.log1p(jnp.exp(-jnp.abs(z)))
    ew = jnp.exp(-softplus - 0.5)
    a = jax.nn.sigmoid(a0_ref[...] + _dot_hi(xwa, a2_ref[...]))
    g = _dot_hi(jax.nn.sigmoid(xg), g2_ref[...])
    k2 = k * (1.0 + (a - 1.0) * ka_ref[...])
    kkr = k * kk_ref[...]
    if nvalid != C:
        valid = rows < nvalid
        ew = jnp.where(valid, ew, 0.0)
        k2 = jnp.where(valid, k2, 0.0)
        v = jnp.where(valid, v, 0.0)
        a = jnp.where(valid, a, 0.0)

    ti = lax.broadcasted_iota(jnp.int32, (C, C), 0)
    tj = lax.broadcasted_iota(jnp.int32, (C, C), 1)
    tri = jnp.where(tj <= ti, 1.0, 0.0).astype(bf16)
    cum = _dot_exact_lhs(tri, -ew)
    pt = jnp.exp(cum)
    pprev = jnp.exp(cum + ew)
    pinv = jnp.exp(-cum)
    clast = cum[C - 1:C, :]
    pcinv = jnp.exp(clast - cum)
    pc = jnp.exp(clast)
    rt_all = r * pt
    kt_all = k2 * pinv
    kp_all = k2 * pcinv
    rkk = r * k2 * rk_ref[...]

    lane = lax.broadcasted_iota(jnp.int32, (C, 2 * RWKV_HEAD), 1)
    m0 = lane < RWKV_HEAD
    gi = lax.broadcasted_iota(jnp.int32, (4 * C, 4 * C), 0)
    gj = lax.broadcasted_iota(jnp.int32, (4 * C, 4 * C), 1)
    gil = gi % C
    gjl = gj % C
    gmask = gjl < gil + jnp.where(gi < 2 * C, 0, 1)
    ei = lax.broadcasted_iota(jnp.int32, (2 * C, 2 * C), 0)
    ej = lax.broadcasted_iota(jnp.int32, (2 * C, 2 * C), 1)
    eye = jnp.where(ei == ej, 1.0, 0.0)
    bd = bd_ref[...]

    def stack(x):
        return jnp.concatenate([jnp.where(m0, x, 0.0), jnp.where(m0, 0.0, x)], axis=0)

    def gsum(x):
        hi, lo = _split2(x)
        return jnp.dot(hi, bd, preferred_element_type=f32) + jnp.dot(lo, bd, preferred_element_type=f32)

    nsq = int(math.log2(C))
    PR = range(RWKV_PAIRS)
    lss = [slice(p * 2 * RWKV_HEAD, (p + 1) * 2 * RWKV_HEAD) for p in PR]
    kkp = [kkr[:, ls] for ls in lss]
    nrm = [jnp.sqrt(gsum(x * x)) for x in kkp]
    kk = [kkp[p] / jnp.maximum(nrm[p], 1e-12) for p in PR]
    bb = [kk[p] * a[:, lss[p]] for p in PR]
    ar = [jnp.concatenate([stack(-kk[p] * pprev[:, lss[p]]), stack(rt_all[:, lss[p]])], axis=0).astype(bf16)
          for p in PR]
    bk = [jnp.concatenate([stack(bb[p] * pinv[:, lss[p]]), stack(kt_all[:, lss[p]])], axis=0).astype(bf16)
          for p in PR]
    bkp = [jnp.concatenate([stack(bb[p] * pcinv[:, lss[p]]), stack(kp_all[:, lss[p]])], axis=0).astype(bf16)
           for p in PR]
    v_s = [stack(v[:, ls]) for ls in lss]
    s0 = [s_scr[p] for p in PR]
    gm = [jnp.where(gmask, _dot_nt(ar[p], bk[p]), 0.0) for p in PR]
    arh = [_dot_nt(ar[p], s0[p]) for p in PR]
    rhs = [arh[p][:2 * C] + _dot(gm[p][:2 * C, 2 * C:], v_s[p]) for p in PR]
    rbk = [gm[p][2 * C:, :].astype(bf16) for p in PR]
    npow = [gm[p][:2 * C, :2 * C] for p in PR]
    tinv = [eye + npow[p] for p in PR]
    for i in range(1, nsq):
        npow = [_dot(n, n) for n in npow]
        tinv = [tinv[p] + _dot(npow[p], tinv[p]) for p in PR]
    u_s = [_dot(tinv[p], rhs[p]) for p in PR]
    uv = [jnp.concatenate([u_s[p], v_s[p]], axis=0).astype(bf16) for p in PR]
    y_s = [arh[p][2 * C:] + _dot(rbk[p], uv[p]) for p in PR]
    for p in PR:
        s_scr[p] = s0[p] * pc[:, lss[p]] + _dot_tn(uv[p], bkp[p])
    yp = [y[:C] + y[C:] for y in y_s]
    mean = [gsum(y) * (1.0 / RWKV_HEAD) for y in yp]
    yc = [yp[p] - mean[p] for p in PR]
    var = [gsum(y * y) * (1.0 / RWKV_HEAD) for y in yc]
    bonus = [gsum(rkk[:, ls]) for ls in lss]
    for p in PR:
        ls = lss[p]
        yn = yc[p] * lax.rsqrt(var[p] + RWKV_GN_EPS) * lg_ref[:, ls] + lb_ref[:, ls]
        yo = (yn + bonus[p] * v[:, ls]) * g[:, ls]
        y_ref[:, ls] = yo[:nvalid]

    @pl.when(c == nchunk - 1)
    def _():
        sout_ref[0] = s_scr[...]


def _pair_states(s):
    B = s.shape[0]
    sp = s.reshape(B, RWKV_PAIRS, 2, RWKV_HEAD, RWKV_HEAD)
    z = jnp.zeros_like(sp[:, :, 0])
    top = jnp.concatenate([sp[:, :, 0], z], -1)
    bot = jnp.concatenate([z, sp[:, :, 1]], -1)
    return jnp.concatenate([top, bot], -2)


def _unpair_states(sp):
    B = sp.shape[0]
    h = RWKV_HEAD
    return jnp.stack([sp[:, :, :h, :h], sp[:, :, h:, h:]], 2).reshape(B, RWKV_HEADS, h, h)


def _rwkv(p_rwkv, row0, B, L, sh0, s0, consts):
    nvalid = min(L, CHUNK)
    nchunk = L // nvalid
    rb0 = row0 // nvalid
    full = lambda shape: pl.BlockSpec(shape, lambda b, c: (0,) * len(shape))
    in_specs = [
        pl.BlockSpec((nvalid, RWKV_PROJ_PAD), lambda b, c: (rb0 + b * nchunk + c, 0)),
        pl.BlockSpec((1, 1, RWKV_PROJ_PAD), lambda b, c: (b, 0, 0)),
        pl.BlockSpec((1, RWKV_PAIRS, 2 * RWKV_HEAD, 2 * RWKV_HEAD), lambda b, c: (b, 0, 0, 0)),
    ] + [full(x.shape) for x in consts]
    y, s_out = pl.pallas_call(
        functools.partial(_rwkv_kernel, nvalid=nvalid, nchunk=nchunk),
        grid=(B, nchunk),
        in_specs=in_specs,
        out_specs=[
            pl.BlockSpec((nvalid, RWKV_W), lambda b, c: (b * nchunk + c, 0)),
            pl.BlockSpec((1, RWKV_PAIRS, 2 * RWKV_HEAD, 2 * RWKV_HEAD), lambda b, c: (b, 0, 0, 0)),
        ],
        out_shape=[
            jax.ShapeDtypeStruct((B * L, RWKV_W), f32),
            jax.ShapeDtypeStruct((B, RWKV_PAIRS, 2 * RWKV_HEAD, 2 * RWKV_HEAD), f32),
        ],
        scratch_shapes=[
            pltpu.VMEM((1, RWKV_PROJ_PAD), f32),
            pltpu.VMEM((RWKV_PAIRS, 2 * RWKV_HEAD, 2 * RWKV_HEAD), f32),
        ],
        compiler_params=pltpu.CompilerParams(
            dimension_semantics=("arbitrary", "arbitrary"), vmem_limit_bytes=VMEM_LIMIT),
        name=f"rwkv7_{L}",
    )(p_rwkv, sh0, _pair_states(s0), *consts)
    return y, _unpair_states(s_out)


def _outproj_kernel(yr_ref, yw_ref, x_ref, wr_ref, ww_ref, g_ref, x1_ref, h_ref):
    x1 = (x_ref[...] + jnp.dot(yr_ref[...].astype(bf16), wr_ref[...], preferred_element_type=f32)
          + jnp.dot(yw_ref[...].astype(bf16), ww_ref[...], preferred_element_type=f32))
    x1_ref[...] = x1
    ms = jnp.mean(x1 * x1, axis=-1, keepdims=True)
    h_ref[...] = x1 * lax.rsqrt(ms + EPS) * g_ref[...]


def _outproj(y_ret, y_rwkv, x, w_ret_bf, w_rwkv_bf, g, tm):
    T, D = x.shape
    row = lambda shape: pl.BlockSpec(shape, lambda i: (i, 0))
    full = lambda shape: pl.BlockSpec(shape, lambda i: (0, 0))
    return pl.pallas_call(
        _outproj_kernel,
        grid=(T // tm,),
        in_specs=[row((tm, RET_V_W)), row((tm, RWKV_W)), row((tm, D)),
                  full(w_ret_bf.shape), full(w_rwkv_bf.shape), full((1, D))],
        out_specs=[row((tm, D)), row((tm, D))],
        out_shape=[jax.ShapeDtypeStruct((T, D), f32), jax.ShapeDtypeStruct((T, D), f32)],
        compiler_params=pltpu.CompilerParams(
            dimension_semantics=("parallel",), vmem_limit_bytes=VMEM_LIMIT),
        name=f"outproj_{T}",
    )(y_ret, y_rwkv, x, w_ret_bf, w_rwkv_bf, g)


def _topk_multi(s_list, pos_list, vals_refs, idx_refs, lanes_list):
    n = len(s_list)
    big = 3.0e38
    for it in range(PEER_TOPK):
        m = [jnp.max(s, axis=0, keepdims=True) for s in s_list]
        idx = [jnp.min(jnp.where(s_list[q] == m[q], pos_list[q], big), axis=0, keepdims=True) for q in range(n)]
        for q in range(n):
            vals_refs[q][it:it + 1, lanes_list[q]] = m[q]
            idx_refs[q][it:it + 1, lanes_list[q]] = idx[q]
        s_list = [jnp.where(pos_list[q] == idx[q], -jnp.inf, s_list[q]) for q in range(n)]


def _cand_positions():
    K = PEER_TOPK
    pos = [float(j) for j in range(K)]
    for i in range(1, 8):
        pos += [float(i * K + j) if j < K // (i + 1) else -1.0 for j in range(8)]
    pos += [float(i * K) for i in range(8, K)]
    return np.asarray(pos, np.float32)


N_CAND = 16 + 7 * 8 + 8


def _peer_topk_kernel(h_ref, wq_ref, sk_ref, cpos_ref, e_ref, g_ref, v1, i1, v2, i2, tv, tp, *, tm):
    q = jnp.dot(h_ref[...].astype(bf16), wq_ref[...], preferred_element_type=f32)
    K = PEER_TOPK
    nt = tm // LANES
    tiles = [slice(lt * LANES, (lt + 1) * LANES) for lt in range(nt)]
    key_iota = lax.broadcasted_iota(jnp.int32, (N_KEYS, LANES), 0).astype(f32)
    s_list, vrefs, irefs, lanes_list = [], [], [], []
    for lanes in tiles:
        for half, (vr, ir) in enumerate(((v1, i1), (v2, i2))):
            qh = q[lanes, half * PEER_HALF:(half + 1) * PEER_HALF]
            s_list.append(_dot_nt(sk_ref[0, half], qh))
            vrefs.append(vr)
            irefs.append(ir)
            lanes_list.append(lanes)
    _topk_multi(s_list, [key_iota] * len(s_list), vrefs, irefs, lanes_list)

    cpos = cpos_ref[...]
    cands = []
    for lanes in tiles:
        a = v1[:, lanes]
        b = v2[:, lanes]
        blocks = [a[0:1, :] + b]
        blocks += [a[i:i + 1, :] + b[0:8, :] for i in range(1, 8)]
        blocks.append(a[8:K, :] + b[0:1, :])
        cands.append(jnp.where(cpos >= 0.0, jnp.concatenate(blocks, axis=0), -jnp.inf))
    _topk_multi(cands, [cpos] * nt, [tv] * nt, [tp] * nt, tiles)

    for lanes in tiles:
        pos = tp[:, lanes]
        pi = jnp.floor(pos * (1.0 / K))
        pj = pos - pi * K
        e1 = jnp.zeros((K, LANES), f32)
        e2 = jnp.zeros((K, LANES), f32)
        for i in range(K):
            e1 = e1 + jnp.where(pi == float(i), i1[i:i + 1, lanes], 0.0)
            e2 = e2 + jnp.where(pj == float(i), i2[i:i + 1, lanes], 0.0)
        e_ref[:, lanes] = (e1 * float(N_KEYS) + e2).astype(jnp.int32)
        top = tv[:, lanes]
        ex = jnp.exp(top - top[0:1, :])
        g_ref[:, lanes] = ex / jnp.sum(ex, axis=0, keepdims=True)


def _peer_topk(h2, wq_bf, sk_bf, tm):
    T, D = h2.shape
    K = PEER_TOPK
    cpos = jnp.asarray(np.broadcast_to(_cand_positions()[:, None], (N_CAND, LANES)))
    return pl.pallas_call(
        functools.partial(_peer_topk_kernel, tm=tm),
        grid=(T // tm, PEER_HEADS),
        in_specs=[
            pl.BlockSpec((tm, D), lambda i, h: (i, 0)),
            pl.BlockSpec((D, PEER_DK), lambda i, h: (0, h)),
            pl.BlockSpec((1, 2, N_KEYS, PEER_HALF), lambda i, h: (h, 0, 0, 0)),
            pl.BlockSpec((N_CAND, LANES), lambda i, h: (0, 0)),
        ],
        out_specs=[
            pl.BlockSpec((K, tm), lambda i, h: (h, i)),
            pl.BlockSpec((K, tm), lambda i, h: (h, i)),
        ],
        out_shape=[jax.ShapeDtypeStruct((PEER_SLOTS, T), jnp.int32),
                   jax.ShapeDtypeStruct((PEER_SLOTS, T), f32)],
        scratch_shapes=[pltpu.VMEM((K, tm), f32)] * 6,
        compiler_params=pltpu.CompilerParams(
            dimension_semantics=("parallel", "arbitrary"), vmem_limit_bytes=VMEM_LIMIT),
        name=f"peer_topk_{T}",
    )(h2, wq_bf, sk_bf, cpos)


PEER_NBUF = 8
PEER_GROUP = 8
SUB = 8
DS = D_MODEL // LANES
EROWS = 2 * DS


_ROWSUM8_ORDER = (0, 4, 2, 6, 1, 5, 3, 7)


def _rowsum8(xs, sub_id):
    up = lambda x, k: pltpu.roll(x, SUB - k, 0)
    down = lambda x, k: pltpu.roll(x, k, 0)
    p = [xs[i] for i in _ROWSUM8_ORDER]
    lo4 = sub_id < 4
    t = [jnp.where(lo4, p[2 * k], p[2 * k + 1]) + pltpu.roll(jnp.where(lo4, p[2 * k + 1], p[2 * k]), 4, 0)
         for k in range(4)]
    lo2 = (sub_id % 4) < 2
    z = [jnp.where(lo2, t[2 * k] + up(t[2 * k], 2), t[2 * k + 1] + down(t[2 * k + 1], 2)) for k in range(2)]
    lo1 = (sub_id % 2) < 1
    return jnp.where(lo1, z[0] + up(z[0], 1), z[1] + down(z[1], 1))


def _peer_mix_kernel(idx_ref, idx_next_ref, gate_ref, h_ref, x1_ref, fg_ref, uv_hbm, o_ref, *scratch, tb, nsteps):
    bufs = scratch[:PEER_NBUF]
    z_scr, sem = scratch[PEER_NBUF:]
    S = PEER_SLOTS
    tile = lambda s: slice(s * LANES, (s + 1) * LANES)
    G = PEER_GROUP
    ngroups = tb // G
    step = pl.program_id(0)

    def issue(t, slot, ids=idx_ref):
        for e in range(S):
            pltpu.make_async_copy(uv_hbm.at[ids[t, e]], bufs[slot].at[e], sem.at[slot]).start(priority=e % 2)

    def wait(slot):
        pltpu.make_async_copy(bufs[slot], bufs[slot], sem.at[slot]).wait()

    @pl.when(step == 0)
    def _():
        for t in range(PEER_NBUF - 2):
            issue(t, t)

    fg = fg_ref[...]

    def group(t0, first, last):
        gcols = jnp.transpose(gate_ref[pl.ds(t0, G), :])
        gcol_id = lax.broadcasted_iota(jnp.int32, (S, G), 1)
        sub_id = lax.broadcasted_iota(jnp.int32, (SUB, LANES), 0)
        for pp in range(G // 2):
            js = (2 * pp, 2 * pp + 1)
            for j in js:
                wait(j)
            prev = ((js[0] - 2) % G, (js[1] - 2) % G)
            if pp == 0 and first:
                for j in prev:
                    issue(j, j)
            elif pp == 0 or not last:
                for j in prev:
                    issue(t0 + js[0] - 2 + (j - prev[0]) + G, j)
            else:
                for j in prev:
                    issue(j, j, idx_next_ref)
            x3 = [h_ref[t0 + j] for j in js]
            x1t = [x1_ref[pl.ds(t0 + j, 1), :] for j in js]
            parts = [[], []]
            for g in range(S // SUB):
                for q, j in enumerate(js):
                    prods = []
                    for i in range(SUB):
                        ue = bufs[j][g * SUB + i, 0:DS, :].astype(f32)
                        prods.append(ue[0:SUB] * x3[q][0:SUB] + ue[SUB:DS] * x3[q][SUB:DS])
                    parts[q].append(_rowsum8(prods, sub_id))
            a = [jnp.sum(jnp.concatenate(p, axis=0), axis=1, keepdims=True) for p in parts]
            ab = jnp.where(gcol_id == js[0], a[0], a[1])
            cwab = 0.5 * ab * (1.0 + lax.erf(ab * (2.0 ** -0.5))) * gcols
            cw = [jnp.broadcast_to(jnp.sum(jnp.where(gcol_id == j, cwab, 0.0), axis=1, keepdims=True), (S, LANES))
                  for j in js]
            accs = [[jnp.zeros((SUB, LANES), f32) for _ in range(4)] for _ in js]
            for e in range(S):
                for q, j in enumerate(js):
                    ve = bufs[j][e, DS:EROWS, :].astype(f32)
                    c = jnp.broadcast_to(cw[q][e:e + 1, :], (SUB, LANES))
                    r = 2 * (e % 2)
                    accs[q][r] = accs[q][r] + ve[0:SUB] * c
                    accs[q][r + 1] = accs[q][r + 1] + ve[SUB:DS] * c
            for q, j in enumerate(js):
                out3 = jnp.concatenate([accs[q][0] + accs[q][2], accs[q][1] + accs[q][3]], axis=0)
                for s in range(DS):
                    z_scr[j:j + 1, tile(s)] = x1t[q][:, tile(s)] + out3[s:s + 1, :]
        z = z_scr[...]
        ms = jnp.mean(z * z, axis=-1, keepdims=True)
        o_ref[pl.ds(t0, G), :] = z * lax.rsqrt(ms + EPS) * fg

    group(0, True, False)

    def mid(gi, carry):
        group(pl.multiple_of(gi * G, G), False, False)
        return carry

    lax.fori_loop(1, ngroups - 1, mid, 0)
    group((ngroups - 1) * G, False, True)

    @pl.when(step == nsteps - 1)
    def _():
        for j in range(G - 2):
            wait(j)


def _expert_slabs(expert_u, expert_v):
    E = expert_u.shape[0]
    return jnp.concatenate([expert_u.astype(bf16), expert_v.astype(bf16)], 1).reshape(E, EROWS, LANES)


def _peer_mix(idx, gates, h2, x1, fg, uv, tb):
    T, D = h2.shape
    S = PEER_SLOTS
    assert PEER_GROUP == PEER_NBUF and tb % PEER_GROUP == 0 and tb >= 2 * PEER_GROUP
    rows = pl.BlockSpec((tb, D), lambda i: (i, 0))
    nsteps = T // tb
    G = PEER_GROUP
    return pl.pallas_call(
        functools.partial(_peer_mix_kernel, tb=tb, nsteps=nsteps),
        grid=(nsteps,),
        in_specs=[
            pl.BlockSpec((tb, S), lambda i: (i, 0), memory_space=pltpu.SMEM),
            pl.BlockSpec((G, S), lambda i: (jnp.minimum(i + 1, nsteps - 1) * (tb // G), 0),
                         memory_space=pltpu.SMEM),
            pl.BlockSpec((tb, S), lambda i: (i, 0)),
            pl.BlockSpec((tb, DS, LANES), lambda i: (i, 0, 0)),
            rows,
            pl.BlockSpec((1, D), lambda i: (0, 0)),
            pl.BlockSpec(memory_space=pl.ANY),
        ],
        out_specs=rows,
        out_shape=jax.ShapeDtypeStruct((T, D), f32),
        scratch_shapes=[pltpu.VMEM((S, EROWS, LANES), bf16)] * PEER_NBUF + [
            pltpu.VMEM((PEER_GROUP, D), f32),
            pltpu.SemaphoreType.DMA((PEER_NBUF,)),
        ],
        compiler_params=pltpu.CompilerParams(
            dimension_semantics=("arbitrary",), vmem_limit_bytes=VMEM_LIMIT),
        name=f"peer_mix_{T}",
    )(idx, idx, gates, h2.reshape(T, DS, LANES), x1, fg.reshape(1, D), uv)


def _rwkv_consts(mu, w0, w2, a0, a2, g2, k_k, k_a, r_k, lnx_g, lnx_b):
    pad = RWKV_PROJ_PAD - RWKV_PROJ
    mu_p = jnp.pad(mu, (0, pad)).reshape(1, RWKV_PROJ_PAD)
    w2_p = jnp.concatenate([w2, jnp.zeros((A_LORA, RWKV_W), f32)], 0)
    a2_p = jnp.concatenate([jnp.zeros((W_LORA, RWKV_W), f32), a2], 0)
    g2_p = jnp.concatenate([g2, jnp.zeros((2 * LANES - G_LORA, RWKV_W), f32)], 0)
    hd = np.arange(2 * RWKV_HEAD) // RWKV_HEAD
    bd = jnp.asarray((hd[:, None] == hd[None, :]).astype(np.float32), dtype=bf16)
    r1 = lambda x: x.reshape(1, RWKV_W)
    return (mu_p, r1(w0), r1(a0), w2_p, a2_p, g2_p, r1(k_k), r1(k_a), r1(r_k), r1(lnx_g), r1(lnx_b), bd)


def kernel(x_prompt, x_sample, state_ret, state_rwkv, state_rwkv_shift, norm1_g, w_in, ret_gain, rwkv_mu,
           rwkv_w0, rwkv_w2, rwkv_a0, rwkv_a2, rwkv_g2, rwkv_k_k, rwkv_k_a, rwkv_r_k, lnx_g, lnx_b, w_out,
           norm2_g, peer_w_query, peer_sub_keys, peer_u, peer_v, final_g):
    Bp, Lp, D = x_prompt.shape
    Bs, Ls, _ = x_sample.shape
    past_len = 2048
    pad = RWKV_PROJ_PAD - RWKV_PROJ

    l = 0
    w_in_l = w_in[l]
    w_ret = w_in_l[:, :RET_PROJ].astype(bf16)
    w_rwkv = jnp.pad(w_in_l[:, RET_PROJ:], ((0, 0), (0, pad))).astype(bf16)
    g1 = norm1_g[l].reshape(1, D)
    gain = ret_gain[l].reshape(1, RET_V_W)
    consts = _rwkv_consts(rwkv_mu[l], rwkv_w0[l], rwkv_w2[l], rwkv_a0[l], rwkv_a2[l], rwkv_g2[l], rwkv_k_k[l],
                          rwkv_k_a[l], rwkv_r_k[l].reshape(-1), lnx_g[l], lnx_b[l])
    w_out_l = w_out[l].astype(bf16)
    g2 = norm2_g[l].reshape(1, D)
    wq = peer_w_query[l].astype(bf16)
    sk = peer_sub_keys[l].astype(bf16)
    uv = _expert_slabs(peer_u[l], peer_v[l])

    def layer(xg, pos0, s_ret, s_rwkv, shift, ret_block, tm):
        B, L, _ = xg.shape
        x = xg.reshape(B * L, D)
        p_ret = _inproj(x, g1, w_ret, tm, RET_PROJ // 3)
        p_rwkv = _inproj(x, g1, w_rwkv, tm, 512)
        y_ret, s_ret_new = _retention(p_ret, 0, B, L, ret_block, pos0, s_ret, gain)
        y_rwkv, s_rwkv_new = _rwkv(p_rwkv, 0, B, L, jnp.pad(shift, ((0, 0), (0, 0), (0, pad))), s_rwkv, consts)
        x1, h2 = _outproj(y_ret, y_rwkv, x, w_out_l[:RET_V_W], w_out_l[RET_V_W:], g2, 512)
        e_t, g_t = _peer_topk(h2, wq, sk, 512)
        y = _peer_mix(e_t.T, g_t.T, h2, x1, final_g, uv, 64)
        shift_new = p_rwkv.reshape(B, L, RWKV_PROJ_PAD)[:, -1:, :RWKV_PROJ]
        return y.reshape(B, L, D), s_ret_new[None], s_rwkv_new[None], shift_new[None]

    y_p, sr_p, sw_p, sh_p = layer(x_prompt, 0, jnp.zeros((Bp, RET_HEADS, RET_DK, RET_DV), f32),
                                  jnp.zeros((Bp, RWKV_HEADS, RWKV_HEAD, RWKV_HEAD), f32),
                                  jnp.zeros((Bp, 1, RWKV_PROJ), f32), 256, 1024)
    y_s, sr_s, sw_s, sh_s = layer(x_sample, past_len, state_ret[l], state_rwkv[l], state_rwkv_shift[l], Ls, 1024)
    return (y_p, y_s, sr_p, sw_p, sh_p, sr_s, sw_s, sh_s)
```

```python
import functools
import math

import numpy as np
import jax
import jax.numpy as jnp
from jax import lax
from jax.experimental import pallas as pl
from jax.experimental.pallas import tpu as pltpu

f32 = jnp.float32
bf16 = jnp.bfloat16

D_MODEL = 2048
CHUNK = 64
EPS = 1e-6
RET_HEADS = 4
RET_DK = 128
RET_DV = 256
RET_QK_W = RET_HEADS * RET_DK
RET_V_W = RET_HEADS * RET_DV
RET_PROJ = 2 * RET_QK_W + 2 * RET_V_W
ROPE_BASE = 10000.0
RWKV_HEAD = 64
RWKV_W = D_MODEL // 2
RWKV_HEADS = RWKV_W // RWKV_HEAD
RWKV_PAIRS = RWKV_HEADS // 2
W_LORA = 64
A_LORA = 64
G_LORA = 160
RWKV_PROJ = 3 * RWKV_W + W_LORA + A_LORA + G_LORA
RWKV_PROJ_PAD = 3584
RWKV_GN_EPS = 64e-5
PEER_HEADS = 8
N_KEYS = 128
PEER_DK = 256
PEER_HALF = PEER_DK // 2
PEER_TOPK = 16
PEER_SLOTS = PEER_HEADS * PEER_TOPK

LANES = 128
VMEM_LIMIT = 56 * 1024 * 1024


def _dot(a, b):
    return jnp.dot(a.astype(bf16), b.astype(bf16), preferred_element_type=f32)


def _dot_nt(a, b):
    return lax.dot_general(a.astype(bf16), b.astype(bf16), (((1,), (1,)), ((), ())),
                           preferred_element_type=f32)


def _dot_tn(a, b):
    return lax.dot_general(a.astype(bf16), b.astype(bf16), (((0,), (0,)), ((), ())),
                           preferred_element_type=f32)


def _split2(x):
    hi = x.astype(bf16)
    lo = (x - hi.astype(f32)).astype(bf16)
    return hi, lo


def _split3(x):
    hi = x.astype(bf16)
    r1 = x - hi.astype(f32)
    mid = r1.astype(bf16)
    lo = (r1 - mid.astype(f32)).astype(bf16)
    return hi, mid, lo


def _dot_exact_rhs(a, b_bf):
    hi, mid, lo = _split3(a)
    return (jnp.dot(hi, b_bf, preferred_element_type=f32)
            + jnp.dot(mid, b_bf, preferred_element_type=f32)
            + jnp.dot(lo, b_bf, preferred_element_type=f32))


def _dot_exact_lhs(a_bf, b):
    hi, mid, lo = _split3(b)
    return (jnp.dot(a_bf, hi, preferred_element_type=f32)
            + jnp.dot(a_bf, mid, preferred_element_type=f32)
            + jnp.dot(a_bf, lo, preferred_element_type=f32))


def _dot_hi(a, b):
    ah, al = _split2(a)
    bh, bl = _split2(b)
    return (jnp.dot(ah, bh, preferred_element_type=f32)
            + jnp.dot(ah, bl, preferred_element_type=f32)
            + jnp.dot(al, bh, preferred_element_type=f32))


def _inproj_kernel(x_ref, g_ref, w_ref, o_ref, h_ref):
    @pl.when(pl.program_id(1) == 0)
    def _():
        x = x_ref[...]
        ms = jnp.mean(x * x, axis=-1, keepdims=True)
        h_ref[...] = (x * lax.rsqrt(ms + EPS) * g_ref[...]).astype(bf16)

    o_ref[...] = jnp.dot(h_ref[...], w_ref[...], preferred_element_type=f32)


def _inproj(x, g, w_bf, tm, tn):
    T, D = x.shape
    N = w_bf.shape[1]
    return pl.pallas_call(
        _inproj_kernel,
        grid=(T // tm, N // tn),
        in_specs=[
            pl.BlockSpec((tm, D), lambda i, j: (i, 0)),
            pl.BlockSpec((1, D), lambda i, j: (0, 0)),
            pl.BlockSpec((D, tn), lambda i, j: (0, j)),
        ],
        out_specs=pl.BlockSpec((tm, tn), lambda i, j: (i, j)),
        out_shape=jax.ShapeDtypeStruct((T, N), f32),
        scratch_shapes=[pltpu.VMEM((tm, D), bf16)],
        compiler_params=pltpu.CompilerParams(
            dimension_semantics=("parallel", "arbitrary"), vmem_limit_bytes=VMEM_LIMIT),
        name=f"inproj_{T}x{N}",
    )(x, g, w_bf)


def _ret_kernel(q_ref, k_ref, v_ref, gt_ref, cs_ref, sn_ref, mask_ref, dec_ref, gain_ref, s0_ref,
                y_ref, sout_ref, s_scr, *, nblk):
    c = pl.program_id(1)

    @pl.when(c == 0)
    def _():
        s_scr[...] = s0_ref[0]

    HR = range(RET_HEADS)
    cs = cs_ref[...]
    sn = sn_ref[...]
    half = RET_DK // 2
    ksl = [slice(h * RET_DK, (h + 1) * RET_DK) for h in HR]
    vsl = [slice(h * RET_DV, (h + 1) * RET_DV) for h in HR]
    q = [q_ref[:, ksl[h]] for h in HR]
    k = [k_ref[:, ksl[h]] for h in HR]
    qr = [q[h] * cs + pltpu.roll(q[h], half, 1) * sn for h in HR]
    kr = [(k[h] * cs + pltpu.roll(k[h], half, 1) * sn) * (RET_DK ** -0.5) for h in HR]
    dec = [dec_ref[h] for h in HR]
    v = [v_ref[:, vsl[h]].astype(bf16) for h in HR]
    s = [s_scr[h] for h in HR]
    sc = [_dot_nt(qr[h], kr[h]) * mask_ref[h] for h in HR]
    o = [_dot(sc[h], v[h]) + _dot(qr[h] * dec[h][:, 0:1], s[h]) for h in HR]
    s_new = [dec[h][0:1, 2:3] * s[h] + _dot_tn(kr[h] * dec[h][:, 1:2], v[h]) for h in HR]
    for h in HR:
        s_scr[h] = s_new[h]
        ms = jnp.mean(o[h] * o[h], axis=-1, keepdims=True)
        on = o[h] * lax.rsqrt(ms + EPS) * gain_ref[:, vsl[h]]
        gt = gt_ref[:, vsl[h]]
        y_ref[:, vsl[h]] = gt * jax.nn.sigmoid(gt) * on

    @pl.when(c == nblk - 1)
    def _():
        sout_ref[0] = s_scr[...]


def _ret_tables(L, cb, pos0, chunked):
    half = RET_DK // 2
    inv = ROPE_BASE ** (-jnp.arange(half, dtype=f32) / half)
    pos = (pos0 + jnp.arange(L, dtype=jnp.int32)).astype(f32)
    ang = pos[:, None] * inv[None, :]
    cos = jnp.cos(ang)
    sin = jnp.sin(ang)
    cs = jnp.concatenate([cos, cos], -1)
    sn = jnp.concatenate([-sin, sin], -1)
    log_gamma = jnp.log1p(-jnp.exp2(-5.0 - jnp.arange(RET_HEADS, dtype=f32)))
    idx = jnp.arange(cb, dtype=f32)
    dist = jnp.abs(idx[:, None] - idx[None, :])
    mask = jnp.exp(log_gamma[:, None, None] * dist[None])
    if chunked:
        ci = jnp.arange(cb) // CHUNK
        mask = jnp.where((ci[None, :] <= ci[:, None])[None], mask, 0.0)
    qdec = jnp.exp(log_gamma[:, None] * (idx[None, :] + 1.0))
    kdec = jnp.exp(log_gamma[:, None] * (cb - 1.0 - idx[None, :]))
    gblk = jnp.broadcast_to(jnp.exp(log_gamma * cb)[:, None], (RET_HEADS, cb))
    dec = jnp.zeros((RET_HEADS, cb, LANES), f32)
    dec = dec.at[:, :, 0].set(qdec).at[:, :, 1].set(kdec).at[:, :, 2].set(gblk)
    return cs, sn, mask, dec


def _retention(p_ret, row0, B, L, cb, pos0, s0, gain):
    nblk = L // cb
    rb0 = row0 // cb
    cs, sn, mask, dec = _ret_tables(L, cb, pos0, chunked=L > CHUNK)
    row = lambda b, c: rb0 + b * nblk + c
    y, s_out = pl.pallas_call(
        functools.partial(_ret_kernel, nblk=nblk),
        grid=(B, nblk),
        in_specs=[
            pl.BlockSpec((cb, RET_QK_W), lambda b, c: (row(b, c), 0)),
            pl.BlockSpec((cb, RET_QK_W), lambda b, c: (row(b, c), 1)),
            pl.BlockSpec((cb, RET_V_W), lambda b, c: (row(b, c), 1)),
            pl.BlockSpec((cb, RET_V_W), lambda b, c: (row(b, c), 2)),
            pl.BlockSpec((cb, RET_DK), lambda b, c: (c, 0)),
            pl.BlockSpec((cb, RET_DK), lambda b, c: (c, 0)),
            pl.BlockSpec((RET_HEADS, cb, cb), lambda b, c: (0, 0, 0)),
            pl.BlockSpec((RET_HEADS, cb, LANES), lambda b, c: (0, 0, 0)),
            pl.BlockSpec((1, RET_V_W), lambda b, c: (0, 0)),
            pl.BlockSpec((1, RET_HEADS, RET_DK, RET_DV), lambda b, c: (b, 0, 0, 0)),
        ],
        out_specs=[
            pl.BlockSpec((cb, RET_V_W), lambda b, c: (b * nblk + c, 0)),
            pl.BlockSpec((1, RET_HEADS, RET_DK, RET_DV), lambda b, c: (b, 0, 0, 0)),
        ],
        out_shape=[
            jax.ShapeDtypeStruct((B * L, RET_V_W), f32),
            jax.ShapeDtypeStruct((B, RET_HEADS, RET_DK, RET_DV), f32),
        ],
        scratch_shapes=[pltpu.VMEM((RET_HEADS, RET_DK, RET_DV), f32)],
        compiler_params=pltpu.CompilerParams(
            dimension_semantics=("arbitrary", "arbitrary"), vmem_limit_bytes=VMEM_LIMIT),
        name=f"retention_{L}",
    )(p_ret, p_ret, p_ret, p_ret, cs, sn, mask, dec, gain, s0)
    return y, s_out


def _rwkv_kernel(p_ref, sh0_ref, s0_ref, mu_ref, w0_ref, a0_ref, w2_ref, a2_ref, g2_ref, kk_ref, ka_ref,
                 rk_ref, lg_ref, lb_ref, bd_ref, y_ref, sout_ref, carry, s_scr, *, nvalid, nchunk):
    C = CHUNK
    c = pl.program_id(1)

    @pl.when(c == 0)
    def _():
        carry[...] = sh0_ref[0]
        s_scr[...] = s0_ref[0]

    if nvalid == C:
        pf = p_ref[...]
    else:
        pf = jnp.concatenate([p_ref[...], jnp.zeros((C - nvalid, RWKV_PROJ_PAD), f32)], axis=0)
    rows = lax.broadcasted_iota(jnp.int32, (C, 1), 0)
    prev = jnp.where(rows == 0, carry[...], pltpu.roll(pf, 1, 0))
    pm = pf + (prev - pf) * mu_ref[...]
    carry[...] = pf[nvalid - 1:nvalid, :]

    o1, o2, o3 = RWKV_W, 2 * RWKV_W, 3 * RWKV_W
    r = pm[:, :o1]
    k = pm[:, o1:o2]
    v = pm[:, o2:o3]
    xwa = pm[:, o3:o3 + W_LORA + A_LORA]
    xg = pm[:, o3 + W_LORA + A_LORA:o3 + W_LORA + A_LORA + 2 * LANES]
    z = w0_ref[...] + _dot_hi(jnp.tanh(xwa), w2_ref[...])
    softplus = jnp.maximum(-z, 0.0) + jnp.log1p(jnp.exp(-jnp.abs(z)))
    ew = jnp.exp(-softplus - 0.5)
    a = jax.nn.sigmoid(a0_ref[...] + _dot_hi(xwa, a2_ref[...]))
    g = _dot_hi(jax.nn.sigmoid(xg), g2_ref[...])
    k2 = k * (1.0 + (a - 1.0) * ka_ref[...])
    kkr = k * kk_ref[...]
    if nvalid != C:
        valid = rows < nvalid
        ew = jnp.where(valid, ew, 0.0)
        k2 = jnp.where(valid, k2, 0.0)
        v = jnp.where(valid, v, 0.0)
        a = jnp.where(valid, a, 0.0)

    ti = lax.broadcasted_iota(jnp.int32, (C, C), 0)
    tj = lax.broadcasted_iota(jnp.int32, (C, C), 1)
    tri = jnp.where(tj <= ti, 1.0, 0.0).astype(bf16)
    cum = _dot_exact_lhs(tri, -ew)
    pt = jnp.exp(cum)
    pprev = jnp.exp(cum + ew)
    pinv = jnp.exp(-cum)
    clast = cum[C - 1:C, :]
    pcinv = jnp.exp(clast - cum)
    pc = jnp.exp(clast)
    rt_all = r * pt
    kt_all = k2 * pinv
    kp_all = k2 * pcinv
    rkk = r * k2 * rk_ref[...]

    lane = lax.broadcasted_iota(jnp.int32, (C, 2 * RWKV_HEAD), 1)
    m0 = lane < RWKV_HEAD
    gi = lax.broadcasted_iota(jnp.int32, (4 * C, 4 * C), 0)
    gj = lax.broadcasted_iota(jnp.int32, (4 * C, 4 * C), 1)
    gil = gi % C
    gjl = gj % C
    gmask = gjl < gil + jnp.where(gi < 2 * C, 0, 1)
    ei = lax.broadcasted_iota(jnp.int32, (2 * C, 2 * C), 0)
    ej = lax.broadcasted_iota(jnp.int32, (2 * C, 2 * C), 1)
    eye = jnp.where(ei == ej, 1.0, 0.0)
    bd = bd_ref[...]

    def stack(x):
        return jnp.concatenate([jnp.where(m0, x, 0.0), jnp.where(m0, 0.0, x)], axis=0)

    def gsum(x):
        hi, lo = _split2(x)
        return jnp.dot(hi, bd, preferred_element_type=f32) + jnp.dot(lo, bd, preferred_element_type=f32)

    nsq = int(math.log2(C))
    PR = range(RWKV_PAIRS)
    lss = [slice(p * 2 * RWKV_HEAD, (p + 1) * 2 * RWKV_HEAD) for p in PR]
    kkp = [kkr[:, ls] for ls in lss]
    nrm = [jnp.sqrt(gsum(x * x)) for x in kkp]
    kk = [kkp[p] / jnp.maximum(nrm[p], 1e-12) for p in PR]
    bb = [kk[p] * a[:, lss[p]] for p in PR]
    ar = [jnp.concatenate([stack(-kk[p] * pprev[:, lss[p]]), stack(rt_all[:, lss[p]])], axis=0).astype(bf16)
          for p in PR]
    bk = [jnp.concatenate([stack(bb[p] * pinv[:, lss[p]]), stack(kt_all[:, lss[p]])], axis=0).astype(bf16)
          for p in PR]
    bkp = [jnp.concatenate([stack(bb[p] * pcinv[:, lss[p]]), stack(kp_all[:, lss[p]])], axis=0).astype(bf16)
           for p in PR]
    v_s = [stack(v[:, ls]) for ls in lss]
    s0 = [s_scr[p] for p in PR]
    gm = [jnp.where(gmask, _dot_nt(ar[p], bk[p]), 0.0) for p in PR]
    arh = [_dot_nt(ar[p], s0[p]) for p in PR]
    rhs = [arh[p][:2 * C] + _dot(gm[p][:2 * C, 2 * C:], v_s[p]) for p in PR]
    rbk = [gm[p][2 * C:, :].astype(bf16) for p in PR]
    npow = [gm[p][:2 * C, :2 * C] for p in PR]
    tinv = [eye + npow[p] for p in PR]
    for i in range(1, nsq):
        npow = [_dot(n, n) for n in npow]
        tinv = [tinv[p] + _dot(npow[p], tinv[p]) for p in PR]
    u_s = [_dot(tinv[p], rhs[p]) for p in PR]
    uv = [jnp.concatenate([u_s[p], v_s[p]], axis=0).astype(bf16) for p in PR]
    y_s = [arh[p][2 * C:] + _dot(rbk[p], uv[p]) for p in PR]
    for p in PR:
        s_scr[p] = s0[p] * pc[:, lss[p]] + _dot_tn(uv[p], bkp[p])
    yp = [y[:C] + y[C:] for y in y_s]
    mean = [gsum(y) * (1.0 / RWKV_HEAD) for y in yp]
    yc = [yp[p] - mean[p] for p in PR]
    var = [gsum(y * y) * (1.0 / RWKV_HEAD) for y in yc]
    bonus = [gsum(rkk[:, ls]) for ls in lss]
    for p in PR:
        ls = lss[p]
        yn = yc[p] * lax.rsqrt(var[p] + RWKV_GN_EPS) * lg_ref[:, ls] + lb_ref[:, ls]
        yo = (yn + bonus[p] * v[:, ls]) * g[:, ls]
        y_ref[:, ls] = yo[:nvalid]

    @pl.when(c == nchunk - 1)
    def _():
        sout_ref[0] = s_scr[...]


def _pair_states(s):
    B = s.shape[0]
    sp = s.reshape(B, RWKV_PAIRS, 2, RWKV_HEAD, RWKV_HEAD)
    z = jnp.zeros_like(sp[:, :, 0])
    top = jnp.concatenate([sp[:, :, 0], z], -1)
    bot = jnp.concatenate([z, sp[:, :, 1]], -1)
    return jnp.concatenate([top, bot], -2)


def _unpair_states(sp):
    B = sp.shape[0]
    h = RWKV_HEAD
    return jnp.stack([sp[:, :, :h, :h], sp[:, :, h:, h:]], 2).reshape(B, RWKV_HEADS, h, h)


def _rwkv(p_rwkv, row0, B, L, sh0, s0, consts):
    nvalid = min(L, CHUNK)
    nchunk = L // nvalid
    rb0 = row0 // nvalid
    full = lambda shape: pl.BlockSpec(shape, lambda b, c: (0,) * len(shape))
    in_specs = [
        pl.BlockSpec((nvalid, RWKV_PROJ_PAD), lambda b, c: (rb0 + b * nchunk + c, 0)),
        pl.BlockSpec((1, 1, RWKV_PROJ_PAD), lambda b, c: (b, 0, 0)),
        pl.BlockSpec((1, RWKV_PAIRS, 2 * RWKV_HEAD, 2 * RWKV_HEAD), lambda b, c: (b, 0, 0, 0)),
    ] + [full(x.shape) for x in consts]
    y, s_out = pl.pallas_call(
        functools.partial(_rwkv_kernel, nvalid=nvalid, nchunk=nchunk),
        grid=(B, nchunk),
        in_specs=in_specs,
        out_specs=[
            pl.BlockSpec((nvalid, RWKV_W), lambda b, c: (b * nchunk + c, 0)),
            pl.BlockSpec((1, RWKV_PAIRS, 2 * RWKV_HEAD, 2 * RWKV_HEAD), lambda b, c: (b, 0, 0, 0)),
        ],
        out_shape=[
            jax.ShapeDtypeStruct((B * L, RWKV_W), f32),
            jax.ShapeDtypeStruct((B, RWKV_PAIRS, 2 * RWKV_HEAD, 2 * RWKV_HEAD), f32),
        ],
        scratch_shapes=[
            pltpu.VMEM((1, RWKV_PROJ_PAD), f32),
            pltpu.VMEM((RWKV_PAIRS, 2 * RWKV_HEAD, 2 * RWKV_HEAD), f32),
        ],
        compiler_params=pltpu.CompilerParams(
            dimension_semantics=("arbitrary", "arbitrary"), vmem_limit_bytes=VMEM_LIMIT),
        name=f"rwkv7_{L}",
    )(p_rwkv, sh0, _pair_states(s0), *consts)
    return y, _unpair_states(s_out)


def _outproj_kernel(yr_ref, yw_ref, x_ref, wr_ref, ww_ref, g_ref, x1_ref, h_ref):
    x1 = (x_ref[...] + jnp.dot(yr_ref[...].astype(bf16), wr_ref[...], preferred_element_type=f32)
          + jnp.dot(yw_ref[...].astype(bf16), ww_ref[...], preferred_element_type=f32))
    x1_ref[...] = x1
    ms = jnp.mean(x1 * x1, axis=-1, keepdims=True)
    h_ref[...] = x1 * lax.rsqrt(ms + EPS) * g_ref[...]


def _outproj(y_ret, y_rwkv, x, w_ret_bf, w_rwkv_bf, g, tm):
    T, D = x.shape
    row = lambda shape: pl.BlockSpec(shape, lambda i: (i, 0))
    full = lambda shape: pl.BlockSpec(shape, lambda i: (0, 0))
    return pl.pallas_call(
        _outproj_kernel,
        grid=(T // tm,),
        in_specs=[row((tm, RET_V_W)), row((tm, RWKV_W)), row((tm, D)),
                  full(w_ret_bf.shape), full(w_rwkv_bf.shape), full((1, D))],
        out_specs=[row((tm, D)), row((tm, D))],
        out_shape=[jax.ShapeDtypeStruct((T, D), f32), jax.ShapeDtypeStruct((T, D), f32)],
        compiler_params=pltpu.CompilerParams(
            dimension_semantics=("parallel",), vmem_limit_bytes=VMEM_LIMIT),
        name=f"outproj_{T}",
    )(y_ret, y_rwkv, x, w_ret_bf, w_rwkv_bf, g)


def _topk_multi(s_list, pos_list, vals_refs, idx_refs, lanes_list):
    n = len(s_list)
    big = 3.0e38
    for it in range(PEER_TOPK):
        m = [jnp.max(s, axis=0, keepdims=True) for s in s_list]
        idx = [jnp.min(jnp.where(s_list[q] == m[q], pos_list[q], big), axis=0, keepdims=True) for q in range(n)]
        for q in range(n):
            vals_refs[q][it:it + 1, lanes_list[q]] = m[q]
            idx_refs[q][it:it + 1, lanes_list[q]] = idx[q]
        s_list = [jnp.where(pos_list[q] == idx[q], -jnp.inf, s_list[q]) for q in range(n)]


def _cand_positions():
    K = PEER_TOPK
    pos = [float(j) for j in range(K)]
    for i in range(1, 8):
        pos += [float(i * K + j) if j < K // (i + 1) else -1.0 for j in range(8)]
    pos += [float(i * K) for i in range(8, K)]
    return np.asarray(pos, np.float32)


N_CAND = 16 + 7 * 8 + 8


def _peer_topk_kernel(h_ref, wq_ref, sk_ref, cpos_ref, e_ref, g_ref, v1, i1, v2, i2, tv, tp, *, tm):
    q = jnp.dot(h_ref[...].astype(bf16), wq_ref[...], preferred_element_type=f32)
    K = PEER_TOPK
    nt = tm // LANES
    tiles = [slice(lt * LANES, (lt + 1) * LANES) for lt in range(nt)]
    key_iota = lax.broadcasted_iota(jnp.int32, (N_KEYS, LANES), 0).astype(f32)
    s_list, vrefs, irefs, lanes_list = [], [], [], []
    for lanes in tiles:
        for half, (vr, ir) in enumerate(((v1, i1), (v2, i2))):
            qh = q[lanes, half * PEER_HALF:(half + 1) * PEER_HALF]
            s_list.append(_dot_nt(sk_ref[0, half], qh))
            vrefs.append(vr)
            irefs.append(ir)
            lanes_list.append(lanes)
    _topk_multi(s_list, [key_iota] * len(s_list), vrefs, irefs, lanes_list)

    cpos = cpos_ref[...]
    cands = []
    for lanes in tiles:
        a = v1[:, lanes]
        b = v2[:, lanes]
        blocks = [a[0:1, :] + b]
        blocks += [a[i:i + 1, :] + b[0:8, :] for i in range(1, 8)]
        blocks.append(a[8:K, :] + b[0:1, :])
        cands.append(jnp.where(cpos >= 0.0, jnp.concatenate(blocks, axis=0), -jnp.inf))
    _topk_multi(cands, [cpos] * nt, [tv] * nt, [tp] * nt, tiles)

    for lanes in tiles:
        pos = tp[:, lanes]
        pi = jnp.floor(pos * (1.0 / K))
        pj = pos - pi * K
        e1 = jnp.zeros((K, LANES), f32)
        e2 = jnp.zeros((K, LANES), f32)
        for i in range(K):
            e1 = e1 + jnp.where(pi == float(i), i1[i:i + 1, lanes], 0.0)
            e2 = e2 + jnp.where(pj == float(i), i2[i:i + 1, lanes], 0.0)
        e_ref[:, lanes] = (e1 * float(N_KEYS) + e2).astype(jnp.int32)
        top = tv[:, lanes]
        ex = jnp.exp(top - top[0:1, :])
        g_ref[:, lanes] = ex / jnp.sum(ex, axis=0, keepdims=True)


def _peer_topk(h2, wq_bf, sk_bf, tm):
    T, D = h2.shape
    K = PEER_TOPK
    cpos = jnp.asarray(np.broadcast_to(_cand_positions()[:, None], (N_CAND, LANES)))
    return pl.pallas_call(
        functools.partial(_peer_topk_kernel, tm=tm),
        grid=(T // tm, PEER_HEADS),
        in_specs=[
            pl.BlockSpec((tm, D), lambda i, h: (i, 0)),
            pl.BlockSpec((D, PEER_DK), lambda i, h: (0, h)),
            pl.BlockSpec((1, 2, N_KEYS, PEER_HALF), lambda i, h: (h, 0, 0, 0)),
            pl.BlockSpec((N_CAND, LANES), lambda i, h: (0, 0)),
        ],
        out_specs=[
            pl.BlockSpec((K, tm), lambda i, h: (h, i)),
            pl.BlockSpec((K, tm), lambda i, h: (h, i)),
        ],
        out_shape=[jax.ShapeDtypeStruct((PEER_SLOTS, T), jnp.int32),
                   jax.ShapeDtypeStruct((PEER_SLOTS, T), f32)],
        scratch_shapes=[pltpu.VMEM((K, tm), f32)] * 6,
        compiler_params=pltpu.CompilerParams(
            dimension_semantics=("parallel", "arbitrary"), vmem_limit_bytes=VMEM_LIMIT),
        name=f"peer_topk_{T}",
    )(h2, wq_bf, sk_bf, cpos)


PEER_NBUF = 8
PEER_GROUP = 8
SUB = 8
DS = D_MODEL // LANES
EROWS = 2 * DS


_ROWSUM8_ORDER = (0, 4, 2, 6, 1, 5, 3, 7)


def _rowsum8(xs, sub_id):
    up = lambda x, k: pltpu.roll(x, SUB - k, 0)
    down = lambda x, k: pltpu.roll(x, k, 0)
    p = [xs[i] for i in _ROWSUM8_ORDER]
    lo4 = sub_id < 4
    t = [jnp.where(lo4, p[2 * k], p[2 * k + 1]) + pltpu.roll(jnp.where(lo4, p[2 * k + 1], p[2 * k]), 4, 0)
         for k in range(4)]
    lo2 = (sub_id % 4) < 2
    z = [jnp.where(lo2, t[2 * k] + up(t[2 * k], 2), t[2 * k + 1] + down(t[2 * k + 1], 2)) for k in range(2)]
    lo1 = (sub_id % 2) < 1
    return jnp.where(lo1, z[0] + up(z[0], 1), z[1] + down(z[1], 1))


def _peer_mix_kernel(idx_ref, idx_next_ref, gate_ref, h_ref, x1_ref, fg_ref, uv_hbm, o_ref, *scratch, tb, nsteps):
    bufs = scratch[:PEER_NBUF]
    z_scr, sem = scratch[PEER_NBUF:]
    S = PEER_SLOTS
    tile = lambda s: slice(s * LANES, (s + 1) * LANES)
    G = PEER_GROUP
    ngroups = tb // G
    step = pl.program_id(0)

    def issue(t, slot, ids=idx_ref):
        for e in range(S):
            pltpu.make_async_copy(uv_hbm.at[ids[t, e]], bufs[slot].at[e], sem.at[slot]).start(priority=e % 2)

    def wait(slot):
        pltpu.make_async_copy(bufs[slot], bufs[slot], sem.at[slot]).wait()

    @pl.when(step == 0)
    def _():
        for t in range(PEER_NBUF - 2):
            issue(t, t)

    fg = fg_ref[...]

    def group(t0, first, last):
        gcols = jnp.transpose(gate_ref[pl.ds(t0, G), :])
        gcol_id = lax.broadcasted_iota(jnp.int32, (S, G), 1)
        sub_id = lax.broadcasted_iota(jnp.int32, (SUB, LANES), 0)
        for pp in range(G // 2):
            js = (2 * pp, 2 * pp + 1)
            for j in js:
                wait(j)
            prev = ((js[0] - 2) % G, (js[1] - 2) % G)
            if pp == 0 and first:
                for j in prev:
                    issue(j, j)
            elif pp == 0 or not last:
                for j in prev:
                    issue(t0 + js[0] - 2 + (j - prev[0]) + G, j)
            else:
                for j in prev:
                    issue(j, j, idx_next_ref)
            x3 = [h_ref[t0 + j] for j in js]
            x1t = [x1_ref[pl.ds(t0 + j, 1), :] for j in js]
            parts = [[], []]
            for g in range(S // SUB):
                for q, j in enumerate(js):
                    prods = []
                    for i in range(SUB):
                        ue = bufs[j][g * SUB + i, 0:DS, :].astype(f32)
                        prods.append(ue[0:SUB] * x3[q][0:SUB] + ue[SUB:DS] * x3[q][SUB:DS])
                    parts[q].append(_rowsum8(prods, sub_id))
            a = [jnp.sum(jnp.concatenate(p, axis=0), axis=1, keepdims=True) for p in parts]
            ab = jnp.where(gcol_id == js[0], a[0], a[1])
            cwab = 0.5 * ab * (1.0 + lax.erf(ab * (2.0 ** -0.5))) * gcols
            cw = [jnp.broadcast_to(jnp.sum(jnp.where(gcol_id == j, cwab, 0.0), axis=1, keepdims=True), (S, LANES))
                  for j in js]
            accs = [[jnp.zeros((SUB, LANES), f32) for _ in range(4)] for _ in js]
            for e in range(S):
                for q, j in enumerate(js):
                    ve = bufs[j][e, DS:EROWS, :].astype(f32)
                    c = jnp.broadcast_to(cw[q][e:e + 1, :], (SUB, LANES))
                    r = 2 * (e % 2)
                    accs[q][r] = accs[q][r] + ve[0:SUB] * c
                    accs[q][r + 1] = accs[q][r + 1] + ve[SUB:DS] * c
            for q, j in enumerate(js):
                out3 = jnp.concatenate([accs[q][0] + accs[q][2], accs[q][1] + accs[q][3]], axis=0)
                for s in range(DS):
                    z_scr[j:j + 1, tile(s)] = x1t[q][:, tile(s)] + out3[s:s + 1, :]
        z = z_scr[...]
        ms = jnp.mean(z * z, axis=-1, keepdims=True)
        o_ref[pl.ds(t0, G), :] = z * lax.rsqrt(ms + EPS) * fg

    group(0, True, False)

    def mid(gi, carry):
        group(pl.multiple_of(gi * G, G), False, False)
        return carry

    lax.fori_loop(1, ngroups - 1, mid, 0)
    group((ngroups - 1) * G, False, True)

    @pl.when(step == nsteps - 1)
    def _():
        for j in range(G - 2):
            wait(j)


def _slab_kernel(u_ref, v_ref, o_ref):
    for src, r0 in ((u_ref, 0), (v_ref, DS)):
        x = src[...]
        x3 = jnp.stack([x[:, s * LANES:(s + 1) * LANES] for s in range(DS)], axis=0)
        o_ref[:, r0:r0 + DS, :] = pltpu.einshape("sel->esl", x3).astype(bf16)


def _expert_slabs(expert_u, expert_v, eb=256):
    E, D = expert_u.shape
    rows = pl.BlockSpec((eb, D), lambda i: (i, 0))
    return pl.pallas_call(
        _slab_kernel,
        grid=(E // eb,),
        in_specs=[rows, rows],
        out_specs=pl.BlockSpec((eb, EROWS, LANES), lambda i: (i, 0, 0)),
        out_shape=jax.ShapeDtypeStruct((E, EROWS, LANES), bf16),
        compiler_params=pltpu.CompilerParams(
            dimension_semantics=("parallel",), vmem_limit_bytes=VMEM_LIMIT),
        name="expert_slabs",
    )(expert_u, expert_v)


def _peer_mix(idx, gates, h2, x1, fg, uv, tb):
    T, D = h2.shape
    S = PEER_SLOTS
    assert PEER_GROUP == PEER_NBUF and tb % PEER_GROUP == 0 and tb >= 2 * PEER_GROUP
    rows = pl.BlockSpec((tb, D), lambda i: (i, 0))
    nsteps = T // tb
    G = PEER_GROUP
    return pl.pallas_call(
        functools.partial(_peer_mix_kernel, tb=tb, nsteps=nsteps),
        grid=(nsteps,),
        in_specs=[
            pl.BlockSpec((tb, S), lambda i: (i, 0), memory_space=pltpu.SMEM),
            pl.BlockSpec((G, S), lambda i: (jnp.minimum(i + 1, nsteps - 1) * (tb // G), 0),
                         memory_space=pltpu.SMEM),
            pl.BlockSpec((tb, S), lambda i: (i, 0)),
            pl.BlockSpec((tb, DS, LANES), lambda i: (i, 0, 0)),
            rows,
            pl.BlockSpec((1, D), lambda i: (0, 0)),
            pl.BlockSpec(memory_space=pl.ANY),
        ],
        out_specs=rows,
        out_shape=jax.ShapeDtypeStruct((T, D), f32),
        scratch_shapes=[pltpu.VMEM((S, EROWS, LANES), bf16)] * PEER_NBUF + [
            pltpu.VMEM((PEER_GROUP, D), f32),
            pltpu.SemaphoreType.DMA((PEER_NBUF,)),
        ],
        compiler_params=pltpu.CompilerParams(
            dimension_semantics=("arbitrary",), vmem_limit_bytes=VMEM_LIMIT),
        name=f"peer_mix_{T}",
    )(idx, idx, gates, h2.reshape(T, DS, LANES), x1, fg.reshape(1, D), uv)


def _rwkv_consts(mu, w0, w2, a0, a2, g2, k_k, k_a, r_k, lnx_g, lnx_b):
    pad = RWKV_PROJ_PAD - RWKV_PROJ
    mu_p = jnp.pad(mu, (0, pad)).reshape(1, RWKV_PROJ_PAD)
    w2_p = jnp.concatenate([w2, jnp.zeros((A_LORA, RWKV_W), f32)], 0)
    a2_p = jnp.concatenate([jnp.zeros((W_LORA, RWKV_W), f32), a2], 0)
    g2_p = jnp.concatenate([g2, jnp.zeros((2 * LANES - G_LORA, RWKV_W), f32)], 0)
    hd = np.arange(2 * RWKV_HEAD) // RWKV_HEAD
    bd = jnp.asarray((hd[:, None] == hd[None, :]).astype(np.float32), dtype=bf16)
    r1 = lambda x: x.reshape(1, RWKV_W)
    return (mu_p, r1(w0), r1(a0), w2_p, a2_p, g2_p, r1(k_k), r1(k_a), r1(r_k), r1(lnx_g), r1(lnx_b), bd)


def kernel(x_prompt, x_sample, state_ret, state_rwkv, state_rwkv_shift, norm1_g, w_in, ret_gain, rwkv_mu,
           rwkv_w0, rwkv_w2, rwkv_a0, rwkv_a2, rwkv_g2, rwkv_k_k, rwkv_k_a, rwkv_r_k, lnx_g, lnx_b, w_out,
           norm2_g, peer_w_query, peer_sub_keys, peer_u, peer_v, final_g):
    Bp, Lp, D = x_prompt.shape
    Bs, Ls, _ = x_sample.shape
    past_len = 2048
    pad = RWKV_PROJ_PAD - RWKV_PROJ

    l = 0
    w_in_l = w_in[l]
    w_ret = w_in_l[:, :RET_PROJ].astype(bf16)
    w_rwkv = jnp.pad(w_in_l[:, RET_PROJ:], ((0, 0), (0, pad))).astype(bf16)
    g1 = norm1_g[l].reshape(1, D)
    gain = ret_gain[l].reshape(1, RET_V_W)
    consts = _rwkv_consts(rwkv_mu[l], rwkv_w0[l], rwkv_w2[l], rwkv_a0[l], rwkv_a2[l], rwkv_g2[l], rwkv_k_k[l],
                          rwkv_k_a[l], rwkv_r_k[l].reshape(-1), lnx_g[l], lnx_b[l])
    w_out_l = w_out[l].astype(bf16)
    g2 = norm2_g[l].reshape(1, D)
    wq = peer_w_query[l].astype(bf16)
    sk = peer_sub_keys[l].astype(bf16)
    uv = _expert_slabs(peer_u[l], peer_v[l])

    def layer(xg, pos0, s_ret, s_rwkv, shift, ret_block, tm):
        B, L, _ = xg.shape
        x = xg.reshape(B * L, D)
        p_ret = _inproj(x, g1, w_ret, tm, RET_PROJ // 3)
        p_rwkv = _inproj(x, g1, w_rwkv, tm, 512)
        y_ret, s_ret_new = _retention(p_ret, 0, B, L, ret_block, pos0, s_ret, gain)
        y_rwkv, s_rwkv_new = _rwkv(p_rwkv, 0, B, L, jnp.pad(shift, ((0, 0), (0, 0), (0, pad))), s_rwkv, consts)
        x1, h2 = _outproj(y_ret, y_rwkv, x, w_out_l[:RET_V_W], w_out_l[RET_V_W:], g2, 512)
        e_t, g_t = _peer_topk(h2, wq, sk, 1024)
        y = _peer_mix(e_t.T, g_t.T, h2, x1, final_g, uv, 64)
        shift_new = p_rwkv.reshape(B, L, RWKV_PROJ_PAD)[:, -1:, :RWKV_PROJ]
        return y.reshape(B, L, D), s_ret_new[None], s_rwkv_new[None], shift_new[None]

    y_p, sr_p, sw_p, sh_p = layer(x_prompt, 0, jnp.zeros((Bp, RET_HEADS, RET_DK, RET_DV), f32),
                                  jnp.zeros((Bp, RWKV_HEADS, RWKV_HEAD, RWKV_HEAD), f32),
                                  jnp.zeros((Bp, 1, RWKV_PROJ), f32), 256, 1024)
    y_s, sr_s, sw_s, sh_s = layer(x_sample, past_len, state_ret[l], state_rwkv[l], state_rwkv_shift[l], Ls, 1024)
    return (y_p, y_s, sr_p, sw_p, sh_p, sr_s, sw_s, sh_s)
```
